```python
import math
import jax, jax.numpy as jnp
from jax import lax
import numpy as np

D_MODEL = 1024
BATCH = 8
SEQ = 2048
DEPTH = 2
DEC_BATCH = 4
DEC_SEQ = 4096
PAST_LEN = 128

N_MEM = 256
HY_WIDTH = 512
HY_CONV = 3
HY_EMB = 33
HY_BANDS = (HY_EMB - 1) // 2
HY_ORDER = 64
HY_DECAY_TARGET = 1e-2
HY_FAST_PCT = 0.3
HY_SLOW_PCT = 1.5
SSM_WIDTH = 512
SSM_HEAD_DIM = 64
SSM_HEADS = SSM_WIDTH // SSM_HEAD_DIM
SSM_GROUPS = 2
SSM_STATE = 128
SSM_CONV = 5
SSM_CHUNK = 128
BC_WIDTH = SSM_GROUPS * SSM_STATE
XBC_WIDTH = SSM_WIDTH + 2 * BC_WIDTH
MIX_WIDTH = HY_WIDTH + SSM_WIDTH
IN_WIDTH = 3 * HY_WIDTH + SSM_WIDTH + XBC_WIDTH + 2 * SSM_HEADS
XA_HEADS = 4
XA_HEAD_DIM = D_MODEL // XA_HEADS
D_FF = 4 * D_MODEL
EPS = 1e-5

kernel_name = "hyena_ssd_parallel_encoder"

f32 = jnp.float32


def rmsnorm(x, w):
    xf = x.astype(f32)
    y = xf * lax.rsqrt(jnp.mean(xf * xf, axis=-1, keepdims=True) + EPS) * w.astype(f32)
    return y.astype(x.dtype)


def centred_dwconv(x, w, b):
    K = w.shape[0]
    pad = K // 2
    L = x.shape[1]
    xp = jnp.pad(x, ((0, 0), (pad, pad), (0, 0)))
    return sum(xp[:, k:k + L] * w[k] for k in range(K)) + b


def hyena_filters(L, fw1, fb1, fw2, fb2, fw3, fb3, freq):
    t = jnp.linspace(0.0, 1.0, L, dtype=f32)[:, None]
    w = (2.0 * math.pi / L) * jnp.arange(L, dtype=f32)[:, None]
    f = jnp.linspace(1e-4, HY_BANDS - 1, HY_BANDS, dtype=f32)[None, :]
    feats = jnp.concatenate([t, jnp.cos(f * w), -jnp.sin(f * w)], axis=-1)
    fr = freq.astype(f32)
    h = jnp.sin(fr * (feats @ fw1.astype(f32) + fb1.astype(f32)))
    h = jnp.sin(fr * (h @ fw2.astype(f32) + fb2.astype(f32)))
    h = (h @ fw3.astype(f32) + fb3.astype(f32)).reshape(L, 2, HY_WIDTH)
    max_decay = math.log(HY_DECAY_TARGET) / HY_FAST_PCT
    min_decay = math.log(HY_DECAY_TARGET) / HY_SLOW_PCT
    deltas = jnp.abs(jnp.linspace(min_decay, max_decay, HY_WIDTH, dtype=f32))
    h = h * jnp.exp(-t * deltas)[:, None, :]
    return h[:, 0], h[:, 1]


def bidir_long_conv(u, h_fwd, h_bwd):
    L = u.shape[1]
    k = jnp.concatenate([h_fwd, jnp.zeros((1, h_fwd.shape[1]), f32), h_bwd[:0:-1]], axis=0)
    kf = jnp.fft.rfft(k, n=2 * L, axis=0)
    uf = jnp.fft.rfft(u.astype(f32), n=2 * L, axis=1)
    return jnp.fft.irfft(uf * kf[None], n=2 * L, axis=1)[:, :L]


def segsum_exp(a):
    q = a.shape[-1]
    cs = jnp.cumsum(a, axis=-1)
    diff = cs[..., :, None] - cs[..., None, :]
    mask = jnp.tril(jnp.ones((q, q), dtype=bool))
    return jnp.where(mask, jnp.exp(jnp.where(mask, diff, 0.0)), 0.0)


def ssd_scan(x, dt, A, B, C):
    b, l, h, p = x.shape
    g, n = B.shape[2], B.shape[3]
    r = h // g
    q = SSM_CHUNK
    c = l // q
    xd = (x.astype(f32) * dt[..., None]).reshape(b, c, q, g, r, p)
    dA = (dt * A.astype(f32)).reshape(b, c, q, g, r).transpose(0, 3, 4, 1, 2)
    Bc = B.astype(f32).reshape(b, c, q, g, n)
    Cc = C.astype(f32).reshape(b, c, q, g, n)
    a_cs = jnp.cumsum(dA, axis=-1)
    Lmat = segsum_exp(dA)
    cb = jnp.einsum("bclgn,bcsgn->bgcls", Cc, Bc)
    y_diag = jnp.einsum("bgrcls,bcsgrp->bclgrp", cb[:, :, None] * Lmat, xd)
    decay_states = jnp.exp(a_cs[..., -1:] - a_cs)
    states = jnp.einsum("bcsgn,bgrcs,bcsgrp->bcgrpn", Bc, decay_states, xd)
    chunk_decay = jnp.exp(a_cs[..., -1])

    def step(hstate, inp):
        s, d = inp
        return hstate * d[..., None, None] + s, hstate

    h0 = jnp.zeros((b, g, r, p, n), f32)
    _, prev = lax.scan(step, h0, (jnp.moveaxis(states, 1, 0), jnp.moveaxis(chunk_decay, -1, 0)))
    prev = jnp.moveaxis(prev, 0, 1)
    y_off = jnp.einsum("bclgn,bcgrpn,bgrcl->bclgrp", Cc, prev, jnp.exp(a_cs))
    return (y_diag + y_off).reshape(b, l, h, p)


def hybrid_mixer(u, p, i):
    b, L, _ = u.shape
    proj = u @ p["w_in"][i]
    o1 = 3 * HY_WIDTH
    o2 = o1 + SSM_WIDTH
    o3 = o2 + XBC_WIDTH
    hy = centred_dwconv(proj[..., :o1], p["hy_conv_w"][i], p["hy_conv_b"][i])
    x0, x1, v = jnp.split(hy, 3, axis=-1)
    h_fwd, h_bwd = hyena_filters(L, p["hy_fw1"][i], p["hy_fb1"][i], p["hy_fw2"][i], p["hy_fb2"][i],
                                 p["hy_fw3"][i], p["hy_fb3"][i], p["hy_sin_freq"][i])
    z = (v * x1).astype(f32)
    z = bidir_long_conv(z, h_fwd, h_bwd) + z * p["hy_bias"][i].astype(f32)
    y_hy = rmsnorm(x0.astype(f32) * z, p["hy_norm"][i])
    gate = proj[..., o1:o2]
    xbc = jax.nn.silu(centred_dwconv(proj[..., o2:o3], p["ssm_conv_w"][i], p["ssm_conv_b"][i]))
    xs = xbc[..., :SSM_WIDTH].reshape(b, L, SSM_HEADS, SSM_HEAD_DIM)
    Bm = xbc[..., SSM_WIDTH:SSM_WIDTH + BC_WIDTH].reshape(b, L, SSM_GROUPS, SSM_STATE)
    Cm = xbc[..., SSM_WIDTH + BC_WIDTH:].reshape(b, L, SSM_GROUPS, SSM_STATE)
    dt = jax.nn.softplus(proj[..., o3:].astype(f32).reshape(b, L, 2, SSM_HEADS)
                         + p["ssm_dt_bias"][i].astype(f32))
    A = -jnp.exp(p["ssm_A_log"][i].astype(f32))
    y_f = ssd_scan(xs, dt[:, :, 0], A[0], Bm, Cm)
    y_b = ssd_scan(xs[:, ::-1], dt[:, ::-1, 1], A[1], Bm[:, ::-1], Cm[:, ::-1])[:, ::-1]
    y_ssm = y_f + y_b + xs.astype(f32) * p["ssm_D"][i].astype(f32)[:, None]
    y_ssm = rmsnorm(y_ssm.reshape(b, L, SSM_WIDTH) * jax.nn.silu(gate.astype(f32)), p["ssm_norm"][i])
    y = jnp.concatenate([y_hy, y_ssm], axis=-1).astype(u.dtype)
    return y @ p["w_out"][i]


def memory_cross_attention(u, mem_n, w_q, w_kv, w_o):
    b, L, _ = u.shape
    m = mem_n.shape[1]
    q = (u @ w_q).reshape(b, L, XA_HEADS, XA_HEAD_DIM)
    kv = (mem_n @ w_kv).reshape(b, m, 2, XA_HEADS, XA_HEAD_DIM)
    k, v = kv[:, :, 0], kv[:, :, 1]
    s = jnp.einsum("bqhd,bkhd->bhqk", q.astype(f32), k.astype(f32)) * (XA_HEAD_DIM ** -0.5)
    a = jax.nn.softmax(s, axis=-1)
    o = jnp.einsum("bhqk,bkhd->bqhd", a, v.astype(f32)).reshape(b, L, D_MODEL).astype(u.dtype)
    return o @ w_o


def trunk(x, mem, p):
    for i in range(DEPTH):
        x = x + hybrid_mixer(rmsnorm(x, p["norm_mix"][i]), p, i).astype(x.dtype)
        x = x + memory_cross_attention(rmsnorm(x, p["norm_xattn"][i]), rmsnorm(mem, p["norm_mem"][i]),
                                       p["w_q"][i], p["w_kv"][i], p["w_o"][i]).astype(x.dtype)
        hdn = rmsnorm(x, p["norm_mlp"][i]) @ p["w_up"][i]
        x = x + (jnp.square(jax.nn.relu(hdn)) @ p["w_down"][i]).astype(x.dtype)
    return rmsnorm(x, p["norm_final"])


def setup_inputs(seed: int = 0) -> dict:
    key = jax.random.key(seed)
    ks = iter(jax.random.split(key, 40))

    def nrm(shape, scale):
        return scale * jax.random.normal(next(ks), shape, f32)

    def gain(shape):
        return 1.0 + nrm(shape, 0.02)

    dt0 = jnp.exp(jax.random.uniform(next(ks), (DEPTH, 2, SSM_HEADS), f32, math.log(1e-3), math.log(1e-1)))
    dt_bias = dt0 + jnp.log(-jnp.expm1(-dt0))
    a_log = jnp.log(jax.random.uniform(next(ks), (DEPTH, 2, SSM_HEADS), f32, 1.0, 16.0))
    return {
        "x_prompt": nrm((BATCH, SEQ, D_MODEL), 1.0),
        "x_sample": nrm((DEC_BATCH, DEC_SEQ, D_MODEL), 1.0),
        "mem_prompt": nrm((BATCH, N_MEM, D_MODEL), 1.0),
        "mem_sample": nrm((DEC_BATCH, N_MEM, D_MODEL), 1.0),
        "norm_mix": gain((DEPTH, D_MODEL)),
        "w_in": nrm((DEPTH, D_MODEL, IN_WIDTH), D_MODEL ** -0.5),
        "hy_conv_w": nrm((DEPTH, HY_CONV, 3 * HY_WIDTH), HY_CONV ** -0.5),
        "hy_conv_b": nrm((DEPTH, 3 * HY_WIDTH), 0.02),
        "hy_fw1": nrm((DEPTH, HY_EMB, HY_ORDER), HY_EMB ** -0.5),
        "hy_fb1": nrm((DEPTH, HY_ORDER), 0.02),
        "hy_fw2": nrm((DEPTH, HY_ORDER, HY_ORDER), HY_ORDER ** -0.5),
        "hy_fb2": nrm((DEPTH, HY_ORDER), 0.02),
        "hy_fw3": nrm((DEPTH, HY_ORDER, 2 * HY_WIDTH), HY_ORDER ** -0.5),
        "hy_fb3": nrm((DEPTH, 2 * HY_WIDTH), 0.02),
        "hy_sin_freq": gain((DEPTH, HY_ORDER)),
        "hy_bias": nrm((DEPTH, HY_WIDTH), 0.1),
        "hy_norm": gain((DEPTH, HY_WIDTH)),
        "ssm_conv_w": nrm((DEPTH, SSM_CONV, XBC_WIDTH), SSM_CONV ** -0.5),
        "ssm_conv_b": nrm((DEPTH, XBC_WIDTH), 0.02),
        "ssm_dt_bias": dt_bias,
        "ssm_A_log": a_log,
        "ssm_D": gain((DEPTH, SSM_HEADS)),
        "ssm_norm": gain((DEPTH, SSM_WIDTH)),
        "w_out": nrm((DEPTH, MIX_WIDTH, D_MODEL), MIX_WIDTH ** -0.5),
        "norm_xattn": gain((DEPTH, D_MODEL)),
        "norm_mem": gain((DEPTH, D_MODEL)),
        "w_q": nrm((DEPTH, D_MODEL, D_MODEL), D_MODEL ** -0.5),
        "w_kv": nrm((DEPTH, D_MODEL, 2 * D_MODEL), D_MODEL ** -0.5),
        "w_o": nrm((DEPTH, D_MODEL, D_MODEL), D_MODEL ** -0.5),
        "norm_mlp": gain((DEPTH, D_MODEL)),
        "w_up": nrm((DEPTH, D_MODEL, D_FF), D_MODEL ** -0.5),
        "w_down": nrm((DEPTH, D_FF, D_MODEL), D_FF ** -0.5),
        "norm_final": gain((D_MODEL,)),
    }


def reference(x_prompt, x_sample, mem_prompt, mem_sample, norm_mix, w_in, hy_conv_w, hy_conv_b,
              hy_fw1, hy_fb1, hy_fw2, hy_fb2, hy_fw3, hy_fb3, hy_sin_freq, hy_bias, hy_norm,
              ssm_conv_w, ssm_conv_b, ssm_dt_bias, ssm_A_log, ssm_D, ssm_norm, w_out,
              norm_xattn, norm_mem, w_q, w_kv, w_o, norm_mlp, w_up, w_down, norm_final):
    p = dict(norm_mix=norm_mix, w_in=w_in, hy_conv_w=hy_conv_w, hy_conv_b=hy_conv_b,
             hy_fw1=hy_fw1, hy_fb1=hy_fb1, hy_fw2=hy_fw2, hy_fb2=hy_fb2, hy_fw3=hy_fw3, hy_fb3=hy_fb3,
             hy_sin_freq=hy_sin_freq, hy_bias=hy_bias, hy_norm=hy_norm,
             ssm_conv_w=ssm_conv_w, ssm_conv_b=ssm_conv_b, ssm_dt_bias=ssm_dt_bias, ssm_A_log=ssm_A_log,
             ssm_D=ssm_D, ssm_norm=ssm_norm, w_out=w_out,
             norm_xattn=norm_xattn, norm_mem=norm_mem, w_q=w_q, w_kv=w_kv, w_o=w_o,
             norm_mlp=norm_mlp, w_up=w_up, w_down=w_down, norm_final=norm_final)
    y_prompt = trunk(x_prompt, mem_prompt, p)
    y_sample = trunk(x_sample, mem_sample, p)
    return (y_prompt, y_sample)
```

```python
import functools
import math

import jax
import jax.numpy as jnp
from jax import lax
from jax.experimental import pallas as pl
from jax.experimental.pallas import tpu as pltpu

f32 = jnp.float32
bf16 = jnp.bfloat16

D_MODEL = 1024
DEPTH = 2
N_MEM = 256
HY_WIDTH = 512
HY_CONV = 3
HY_BANDS = 16
HY_ORDER = 64
HY_DECAY_TARGET = 1e-2
HY_FAST_PCT = 0.3
HY_SLOW_PCT = 1.5
SSM_WIDTH = 512
SSM_HEAD_DIM = 64
SSM_HEADS = 8
SSM_GROUPS = 2
SSM_STATE = 128
SSM_CONV = 5
SSM_CHUNK = 128
BC_WIDTH = SSM_GROUPS * SSM_STATE
XBC_WIDTH = SSM_WIDTH + 2 * BC_WIDTH
HY_IN = 3 * HY_WIDTH
MAIN_IN = HY_IN + SSM_WIDTH + XBC_WIDTH
DT_WIDTH = 2 * SSM_HEADS
XA_HEADS = 4
XA_HEAD_DIM = D_MODEL // XA_HEADS
D_FF = 4 * D_MODEL
EPS = 1e-5

LANES = 128
SUBLANES = 8
HALO = SUBLANES
SUB_LEN = 512
SUB_FFT = 2 * SUB_LEN
HY_SLABS = HY_WIDTH // LANES
ROW_TILE = 512
SSD_ROWS = 512
PW_ROWS = 64
VMEM_LIMIT = 56 * 1024 * 1024


def _cparams(*sem):
    return pltpu.CompilerParams(dimension_semantics=sem, vmem_limit_bytes=VMEM_LIMIT)


def _rms(x, w):
    return x * lax.rsqrt(jnp.mean(x * x, axis=-1, keepdims=True) + EPS) * w


def _dot(a, b):
    return jnp.dot(a, b, preferred_element_type=f32)


def _dot_hi(a, b):
    return jnp.dot(a, b, preferred_element_type=f32, precision=lax.Precision.HIGHEST)


def _dot_nt(a, b):
    return lax.dot_general(a, b, (((1,), (1,)), ((), ())), preferred_element_type=f32)


def _dot_tn(a, b):
    return lax.dot_general(a, b, (((0,), (0,)), ((), ())), preferred_element_type=f32)


def _split3(x):
    hi = x.astype(bf16)
    r1 = x - hi.astype(f32)
    mid = r1.astype(bf16)
    lo = (r1 - mid.astype(f32)).astype(bf16)
    return hi, mid, lo


def _silu(x):
    return x / (1.0 + jnp.exp(-x))


def _softplus(x):
    return jnp.maximum(x, 0.0) + jnp.log(1.0 + jnp.exp(-jnp.abs(x)))


def _const_spec(shape):
    nd = len(shape)
    return pl.BlockSpec(shape, lambda *_: (0,) * nd)


def _inproj_kernel(xp_ref, x_ref, xn_ref, nw_ref, w_ref, wdt_ref, wdtt_ref, hcw_ref, hcb_ref,
                   scw_ref, scb_ref, x0_ref, z_ref, gate_ref, xbc_ref, dt_ref, dtt_ref, pbuf):
    tm = x_ref.shape[1]
    i = pl.program_id(1)
    nw = nw_ref[...]
    xp = jnp.where(i > 0, _rms(xp_ref[0], nw), 0.0)
    xn = jnp.where(i < pl.num_programs(1) - 1, _rms(xn_ref[0], nw), 0.0)
    xm = _rms(x_ref[0], nw)
    xe = jnp.concatenate([xp, xm, xn], axis=0).astype(bf16)
    xmb = xe[HALO:HALO + tm]

    def conv_block(col0, cw_ref, cb_ref, taps, wcol0):
        width = pbuf.shape[1]
        pbuf[...] = _dot(xe, w_ref[:, col0:col0 + width])
        pad = taps // 2
        acc = cb_ref[:, wcol0:wcol0 + width]
        for k in range(taps):
            acc = acc + pbuf[pl.ds(HALO + k - pad, tm), :] * cw_ref[k:k + 1, wcol0:wcol0 + width]
        return acc

    x0_ref[0] = conv_block(0, hcw_ref, hcb_ref, HY_CONV, 0)
    x1 = conv_block(HY_WIDTH, hcw_ref, hcb_ref, HY_CONV, HY_WIDTH)
    v = conv_block(2 * HY_WIDTH, hcw_ref, hcb_ref, HY_CONV, 2 * HY_WIDTH)
    z = v * x1
    for s in range(HY_SLABS):
        z_ref[0, s] = z[:, s * LANES:(s + 1) * LANES]
    gate_ref[0] = _dot(xmb, w_ref[:, HY_IN:HY_IN + SSM_WIDTH])
    o2 = HY_IN + SSM_WIDTH
    for c in range(XBC_WIDTH // HY_WIDTH):
        blk = conv_block(o2 + c * HY_WIDTH, scw_ref, scb_ref, SSM_CONV, c * HY_WIDTH)
        xbc_ref[0, :, c * HY_WIDTH:(c + 1) * HY_WIDTH] = _silu(blk)
    dt_ref[0] = _dot(xmb, wdt_ref[...])
    dtt_ref[0] = _dot_nt(wdtt_ref[...], xmb)


def _inproj(x, nw, w_main, w_dt, w_dtt, hcw, hcb, scw, scb):
    b, l, d = x.shape
    tm = ROW_TILE
    nt = l // tm
    hb = tm // HALO
    grid = (b, nt)
    out_shape = (
        jax.ShapeDtypeStruct((b, l, HY_WIDTH), f32),
        jax.ShapeDtypeStruct((b, HY_SLABS, l, LANES), f32),
        jax.ShapeDtypeStruct((b, l, SSM_WIDTH), f32),
        jax.ShapeDtypeStruct((b, l, XBC_WIDTH), f32),
        jax.ShapeDtypeStruct((b, l, DT_WIDTH), f32),
        jax.ShapeDtypeStruct((b, DT_WIDTH, l), f32),
    )
    in_specs = [
        pl.BlockSpec((1, HALO, d), lambda bi, i: (bi, jnp.maximum(i * hb - 1, 0), 0)),
        pl.BlockSpec((1, tm, d), lambda bi, i: (bi, i, 0)),
        pl.BlockSpec((1, HALO, d), lambda bi, i: (bi, jnp.minimum((i + 1) * hb, l // HALO - 1), 0)),
        _const_spec(nw.shape), _const_spec(w_main.shape), _const_spec(w_dt.shape),
        _const_spec(w_dtt.shape), _const_spec(hcw.shape), _const_spec(hcb.shape),
        _const_spec(scw.shape), _const_spec(scb.shape),
    ]
    out_specs = (
        pl.BlockSpec((1, tm, HY_WIDTH), lambda bi, i: (bi, i, 0)),
        pl.BlockSpec((1, HY_SLABS, tm, LANES), lambda bi, i: (bi, 0, i, 0)),
        pl.BlockSpec((1, tm, SSM_WIDTH), lambda bi, i: (bi, i, 0)),
        pl.BlockSpec((1, tm, XBC_WIDTH), lambda bi, i: (bi, i, 0)),
        pl.BlockSpec((1, tm, DT_WIDTH), lambda bi, i: (bi, i, 0)),
        pl.BlockSpec((1, DT_WIDTH, tm), lambda bi, i: (bi, 0, i)),
    )
    return pl.pallas_call(
        _inproj_kernel, grid=grid, in_specs=in_specs, out_specs=out_specs, out_shape=out_shape,
        scratch_shapes=[pltpu.VMEM((tm + 2 * HALO, HY_WIDTH), f32)],
        compiler_params=_cparams("parallel", "parallel"), name="inproj",
    )(x, x, x, nw, w_main, w_dt, w_dtt, hcw, hcb, scw, scb)


def _spectra_kernel(fw1t_ref, fw1c_ref, fw1s_ref, fb1_ref, fw2_ref, fb2_ref, fw3_ref, fb3_ref,
                    freq_ref, tc_ref, ts_ref, kre_ref, kim_ref, *, seq_len, phases):
    d = pl.program_id(0) - (phases - 1)
    n = lax.broadcasted_iota(jnp.int32, (SUB_FFT, 1), 0)
    m = jnp.where(n < SUB_LEN, n, n - SUB_FFT)
    j = phases * m + d
    pos = jnp.abs(j).astype(f32)
    t = pos * (1.0 / (seq_len - 1))
    w = pos * (2.0 * math.pi / seq_len)
    band = lax.broadcasted_iota(jnp.int32, (1, HY_BANDS), 1).astype(f32)
    fband = 1e-4 + band * ((HY_BANDS - 1 - 1e-4) / (HY_BANDS - 1))
    ang = w * fband
    fr = freq_ref[...]
    pre = t * fw1t_ref[...] + _dot_hi(jnp.cos(ang), fw1c_ref[...]) - _dot_hi(jnp.sin(ang), fw1s_ref[...])
    h = jnp.sin(fr * (pre + fb1_ref[...]))
    h = jnp.sin(fr * (_dot_hi(h, fw2_ref[...]) + fb2_ref[...]))
    o = _dot_hi(h, fw3_ref[...]) + fb3_ref[...]
    sel = jnp.where(j >= 0, o[:, :HY_WIDTH], o[:, HY_WIDTH:])
    max_decay = math.log(HY_DECAY_TARGET) / HY_FAST_PCT
    min_decay = math.log(HY_DECAY_TARGET) / HY_SLOW_PCT
    ch = lax.broadcasted_iota(jnp.int32, (1, HY_WIDTH), 1).astype(f32)
    deltas = jnp.abs(min_decay + ch * ((max_decay - min_decay) / (HY_WIDTH - 1)))
    sign = jnp.where(n > SUB_LEN, -1.0, jnp.where(n == SUB_LEN, 0.0, 1.0))
    g = (sel * jnp.exp(-t * deltas) * sign).astype(bf16)
    kre_ref[0] = _dot(tc_ref[...], g)
    kim_ref[0] = _dot(ts_ref[...], g)


def _filter_spectra(seq_len, fw1, fb1, fw2, fb2, fw3, fb3, freq, tc, ts):
    phases = seq_len // SUB_LEN
    nd = 2 * phases - 1
    args = (fw1[0:1], fw1[1:1 + HY_BANDS], fw1[1 + HY_BANDS:], fb1[None], fw2, fb2[None], fw3, fb3[None],
            freq[None], tc, ts)
    out_shape = (jax.ShapeDtypeStruct((nd, SUB_LEN, HY_WIDTH), f32),) * 2
    spec = pl.BlockSpec((1, SUB_LEN, HY_WIDTH), lambda k: (k, 0, 0))
    return pl.pallas_call(
        functools.partial(_spectra_kernel, seq_len=seq_len, phases=phases),
        grid=(nd,), in_specs=[_const_spec(a.shape) for a in args], out_specs=(spec, spec),
        out_shape=out_shape, compiler_params=_cparams("parallel"), name="filter_spectra",
    )(*args)


def _dft_tables():
    f = jnp.arange(SUB_LEN, dtype=jnp.int32)[:, None]
    n = jnp.arange(SUB_FFT, dtype=jnp.int32)[None, :]
    ang = (((2 * f + 1) * n) % (2 * SUB_FFT)).astype(f32) * (math.pi / SUB_FFT)
    c, s = jnp.cos(ang), jnp.sin(ang)
    tc, ts = c.astype(bf16), (-s).astype(bf16)
    scale = 2.0 / SUB_FFT
    ic = (scale * c[:, :SUB_LEN]).T.astype(bf16)
    isn = (-scale * s[:, :SUB_LEN]).T.astype(bf16)
    return tc, ts, ic, isn


def _longconv_kernel(z_ref, kre_ref, kim_ref, tc_ref, ts_ref, ic_ref, is_ref, o_ref,
                     zre, zim, wre, wim, *, phases):
    slabs = z_ref.shape[1]
    for q in range(phases):
        zq = jnp.concatenate(
            [z_ref[0, s, pl.ds(q, SUB_LEN, stride=phases), :] for s in range(slabs)], axis=1).astype(bf16)
        zre[q] = _dot(tc_ref[...], zq)
        zim[q] = _dot(ts_ref[...], zq)
    for p in range(phases):
        def rows(r, carry, p=p):
            rs = pl.ds(pl.multiple_of(r * PW_ROWS, PW_ROWS), PW_ROWS)
            are = jnp.zeros((PW_ROWS, zre.shape[2]), f32)
            aim = jnp.zeros((PW_ROWS, zre.shape[2]), f32)
            for q in range(phases):
                kd = p - q + phases - 1
                kr, ki = kre_ref[kd, rs, :], kim_ref[kd, rs, :]
                xr, xi = zre[q, rs, :], zim[q, rs, :]
                are = are + (kr * xr - ki * xi)
                aim = aim + (kr * xi + ki * xr)
            wre[rs, :] = are.astype(bf16)
            wim[rs, :] = aim.astype(bf16)
            return carry
        lax.fori_loop(0, SUB_LEN // PW_ROWS, rows, 0)
        yp = _dot(ic_ref[...], wre[...]) + _dot(is_ref[...], wim[...])
        for s in range(slabs):
            o_ref[0, s, pl.ds(p, SUB_LEN, stride=phases), :] = yp[:, s * LANES:(s + 1) * LANES]


def _longconv(z4, kre, kim, tcf, tsf, ic, isn):
    b, slabs, l, _ = z4.shape
    phases = l // SUB_LEN
    nd = 2 * phases - 1
    sb = 2
    cb = sb * LANES
    zspec = pl.BlockSpec((1, sb, l, LANES), lambda j, bi: (bi, j, 0, 0))
    kspec = pl.BlockSpec((nd, SUB_LEN, cb), lambda j, bi: (0, 0, j), pipeline_mode=pl.Buffered(1))
    tspec = _const_spec(tcf.shape)
    return pl.pallas_call(
        functools.partial(_longconv_kernel, phases=phases),
        grid=(slabs // sb, b),
        in_specs=[zspec, kspec, kspec, tspec, tspec, tspec, tspec], out_specs=zspec,
        out_shape=jax.ShapeDtypeStruct(z4.shape, f32),
        scratch_shapes=[pltpu.VMEM((phases, SUB_LEN, cb), f32), pltpu.VMEM((phases, SUB_LEN, cb), f32),
                        pltpu.VMEM((SUB_LEN, cb), bf16), pltpu.VMEM((SUB_LEN, cb), bf16)],
        compiler_params=_cparams("parallel", "parallel"), name="longconv",
    )(z4, kre, kim, tcf, tsf, ic, isn)


def _ssd_kernel(*refs, reverse, final):
    if final:
        (xbc_ref, dt_ref, dtt_ref, dtb_r_ref, dtb_c_ref, al_r_ref, al_c_ref, e_ref,
         yf_ref, gate_ref, dskip_ref, nw_ref, y_ref, h_ref) = refs
    else:
        (xbc_ref, dt_ref, dtt_ref, dtb_r_ref, dtb_c_ref, al_r_ref, al_c_ref, e_ref,
         y_ref, h_ref) = refs
    q = SSM_CHUNK
    nch = xbc_ref.shape[1] // q
    col0 = SSM_HEADS if reverse else 0

    @pl.when(pl.program_id(1) == 0)
    def _():
        h_ref[...] = jnp.zeros_like(h_ref)

    ii = lax.broadcasted_iota(jnp.int32, (q, q), 0)
    jj = lax.broadcasted_iota(jnp.int32, (q, q), 1)
    lower = ii >= jj
    m_low = jnp.where(lower, 1.0, 0.0).astype(bf16)
    m_up = jnp.where(ii <= jj, 1.0, 0.0).astype(bf16)
    m_col, m_row, mask = (m_up, m_low, ii <= jj) if reverse else (m_low, m_up, lower)
    a_r = -jnp.exp(al_r_ref[...])
    a_c = -jnp.exp(al_c_ref[...])
    e_mat = e_ref[...]
    hp = (SSM_HEADS // SSM_GROUPS) * SSM_HEAD_DIM

    def chunk(ci, carry):
        c = (nch - 1 - ci) if reverse else ci
        r0 = pl.multiple_of(c * q, q)
        xs = xbc_ref[0, pl.ds(r0, q), 0:SSM_WIDTH]
        bm = xbc_ref[0, pl.ds(r0, q), SSM_WIDTH:SSM_WIDTH + BC_WIDTH].astype(bf16)
        cm = xbc_ref[0, pl.ds(r0, q), SSM_WIDTH + BC_WIDTH:XBC_WIDTH].astype(bf16)
        dtc = _softplus(dt_ref[0, pl.ds(r0, q), :] + dtb_r_ref[...])
        dtr = _softplus(dtt_ref[0, :, pl.ds(r0, q)] + dtb_c_ref[...])
        cs_c = sum(_dot(m_col, part) for part in _split3(dtc * a_r))
        cs_r = sum(_dot(part, m_row) for part in _split3(dtr * a_c))
        e_c = jnp.exp(cs_c)
        edge = cs_c[0:1, :] if reverse else cs_c[q - 1:q, :]
        dec_c = jnp.exp(edge - cs_c)
        stack = jnp.concatenate([dtc, e_c, dec_c], axis=0)
        s_hi = stack.astype(bf16)
        s_lo = (stack - s_hi.astype(f32)).astype(bf16)
        wide = _dot(s_hi, e_mat) + _dot(s_lo, e_mat)
        dt_x, e_x, dec_x = wide[0:q], wide[q:2 * q], wide[2 * q:3 * q]
        xd = xs * dt_x
        xdb = xd.astype(bf16)
        xdd = (xd * dec_x).astype(bf16)
        hprev = h_ref[...]
        hb = hprev.astype(bf16)
        ys, sts = [], []
        for g in range(SSM_GROUPS):
            bg = bm[:, g * SSM_STATE:(g + 1) * SSM_STATE]
            cg = cm[:, g * SSM_STATE:(g + 1) * SSM_STATE]
            gl = slice(g * hp, (g + 1) * hp)
            cb = _dot_nt(cg, bg)
            yoff = _dot(cg, hb[:, gl]) * e_x[:, gl]
            sts.append(_dot_tn(bg, xdd[:, gl]))
            yd = []
            for r in range(SSM_HEADS // SSM_GROUPS):
                hd = g * (SSM_HEADS // SSM_GROUPS) + r
                hc = col0 + hd
                diff = cs_c[:, hc:hc + 1] - cs_r[hc:hc + 1, :]
                lm = jnp.where(mask, jnp.exp(jnp.where(mask, diff, 0.0)), 0.0)
                yd.append(_dot((cb * lm).astype(bf16), xdb[:, hd * SSM_HEAD_DIM:(hd + 1) * SSM_HEAD_DIM]))
            ys.append(jnp.concatenate(yd, axis=1) + yoff)
        y = jnp.concatenate(ys, axis=1)
        edge_x = e_x[0:1, :] if reverse else e_x[q - 1:q, :]
        h_ref[...] = hprev * edge_x + jnp.concatenate(sts, axis=1)
        if final:
            tot = yf_ref[0, pl.ds(r0, q), :] + y + xs * dskip_ref[...]
            gated = tot * _silu(gate_ref[0, pl.ds(r0, q), :])
            y_ref[0, pl.ds(r0, q), :] = _rms(gated, nw_ref[...]).astype(y_ref.dtype)
        else:
            y_ref[0, pl.ds(r0, q), :] = y
        return carry

    lax.fori_loop(0, nch, chunk, 0)


def _ssd(xbc, dt, dtt, dtb, alog, e_mat, *, reverse, final_args=None):
    b, l, _ = xbc.shape
    tr = SSD_ROWS
    nt = l // tr
    final = final_args is not None

    def blk(i):
        return (nt - 1 - i) if reverse else i

    row3 = lambda w: pl.BlockSpec((1, tr, w), lambda bi, i: (bi, blk(i), 0))
    small = (dtb.reshape(1, DT_WIDTH), dtb.reshape(DT_WIDTH, 1), alog.reshape(1, DT_WIDTH),
             alog.reshape(DT_WIDTH, 1), e_mat)
    args = [xbc, dt, dtt, *small]
    in_specs = [row3(XBC_WIDTH), row3(DT_WIDTH),
                pl.BlockSpec((1, DT_WIDTH, tr), lambda bi, i: (bi, 0, blk(i))),
                *[_const_spec(a.shape) for a in small]]
    if final:
        yf, gate, dskip, nw = final_args
        args += [yf, gate, dskip, nw]
        in_specs += [row3(SSM_WIDTH), row3(SSM_WIDTH), _const_spec(dskip.shape), _const_spec(nw.shape)]
    return pl.pallas_call(
        functools.partial(_ssd_kernel, reverse=reverse, final=final),
        grid=(b, nt), in_specs=in_specs, out_specs=row3(SSM_WIDTH),
        out_shape=jax.ShapeDtypeStruct((b, l, SSM_WIDTH), bf16 if final else f32),
        scratch_shapes=[pltpu.VMEM((SSM_STATE, SSM_WIDTH), f32)],
        compiler_params=_cparams("parallel", "arbitrary"), name="ssd_bwd" if reverse else "ssd_fwd",
    )(*args)


def _kv_kernel(mem_ref, nw_ref, wkv_ref, k_ref, v_ref):
    mn = _rms(mem_ref[0], nw_ref[...]).astype(bf16)
    k_ref[0] = _dot(mn, wkv_ref[:, :D_MODEL]).astype(bf16)
    v_ref[0] = _dot(mn, wkv_ref[:, D_MODEL:]).astype(bf16)


def _kv_proj(mem, nw, wkv):
    b, m, d = mem.shape
    spec = pl.BlockSpec((1, m, d), lambda bi: (bi, 0, 0))
    return pl.pallas_call(
        _kv_kernel, grid=(b,), in_specs=[spec, _const_spec(nw.shape), _const_spec(wkv.shape)],
        out_specs=(spec, spec), out_shape=(jax.ShapeDtypeStruct((b, m, d), bf16),) * 2,
        compiler_params=_cparams("parallel"), name="kv_proj",
    )(mem, nw, wkv)


def _mix_attn_kernel(x_ref, x0_ref, z_ref, zc_ref, ys_ref, hb_ref, hn_ref, wout_ref, nx_ref, wq_ref,
                     k_ref, v_ref, wo_ref, o_ref):
    z = jnp.concatenate([z_ref[0, s] for s in range(HY_SLABS)], axis=1)
    zc = jnp.concatenate([zc_ref[0, s] for s in range(HY_SLABS)], axis=1)
    y_hy = _rms(x0_ref[0] * (zc + z * hb_ref[...]), hn_ref[...]).astype(bf16)
    x1 = x_ref[0] + _dot(y_hy, wout_ref[:HY_WIDTH, :]) + _dot(ys_ref[0], wout_ref[HY_WIDTH:, :])
    xn = _rms(x1, nx_ref[...]).astype(bf16)
    qa = (_dot(xn, wq_ref[...]) * (XA_HEAD_DIM ** -0.5)).astype(bf16)
    heads = []
    for h in range(XA_HEADS):
        hs = slice(h * XA_HEAD_DIM, (h + 1) * XA_HEAD_DIM)
        s = _dot_nt(qa[:, hs], k_ref[0, :, hs])
        e = jnp.exp(s - jnp.max(s, axis=-1, keepdims=True))
        den = jnp.sum(e, axis=-1, keepdims=True)
        heads.append(_dot(e.astype(bf16), v_ref[0, :, hs]) / den)
    o = jnp.concatenate(heads, axis=1).astype(bf16)
    o_ref[0] = x1 + _dot(o, wo_ref[...])


def _mix_attn(x, x0, z4, zc4, ys, hbias, hnorm, wout, nx, wq, k, v, wo):
    b, l, d = x.shape
    tm = ROW_TILE
    row = lambda w: pl.BlockSpec((1, tm, w), lambda bi, i: (bi, i, 0))
    slab = pl.BlockSpec((1, HY_SLABS, tm, LANES), lambda bi, i: (bi, 0, i, 0))
    memspec = pl.BlockSpec((1, N_MEM, d), lambda bi, i: (bi, 0, 0))
    consts = (hbias, hnorm, wout, nx, wq)
    return pl.pallas_call(
        _mix_attn_kernel, grid=(b, l // tm),
        in_specs=[row(d), row(HY_WIDTH), slab, slab, row(SSM_WIDTH), *[_const_spec(a.shape) for a in consts],
                  memspec, memspec, _const_spec(wo.shape)],
        out_specs=row(d), out_shape=jax.ShapeDtypeStruct((b, l, d), f32),
        compiler_params=_cparams("parallel", "parallel"), name="mix_attn",
    )(x, x0, z4, zc4, ys, *consts, k, v, wo)


def _mlp_kernel(x_ref, nw_ref, wup_ref, wdn_ref, fw_ref, o_ref, *, final):
    x = x_ref[0]
    xn = _rms(x, nw_ref[...]).astype(bf16)
    acc = x
    for c in range(D_FF // D_MODEL):
        cs = slice(c * D_MODEL, (c + 1) * D_MODEL)
        hdn = jnp.maximum(_dot(xn, wup_ref[:, cs]), 0.0)
        acc = acc + _dot((hdn * hdn).astype(bf16), wdn_ref[cs, :])
    o_ref[0] = _rms(acc, fw_ref[...]) if final else acc


def _mlp(x, nw, wup, wdn, fw, final):
    b, l, d = x.shape
    tm = ROW_TILE
    row = pl.BlockSpec((1, tm, d), lambda bi, i: (bi, i, 0))
    wspec = lambda a: pl.BlockSpec(a.shape, lambda bi, i: (0, 0), pipeline_mode=pl.Buffered(1))
    return pl.pallas_call(
        functools.partial(_mlp_kernel, final=final), grid=(b, l // tm),
        in_specs=[row, _const_spec(nw.shape), wspec(wup), wspec(wdn), _const_spec(fw.shape)],
        out_specs=row, out_shape=jax.ShapeDtypeStruct((b, l, d), f32),
        compiler_params=_cparams("parallel", "parallel"), name="mlp",
    )(x, nw, wup, wdn, fw)


def _trunk(x, mem, p, tables):
    tc, ts, ic, isn = tables
    seq_len = x.shape[1]
    for i in range(DEPTH):
        lp = p["layers"][i]
        x0, z4, gate, xbc, dt, dtt = _inproj(x, lp["norm_mix"], lp["w_main"], lp["w_dt"], lp["w_dtt"],
                                             lp["hy_conv_w"], lp["hy_conv_b"], lp["ssm_conv_w"], lp["ssm_conv_b"])
        kre, kim = _filter_spectra(seq_len, lp["hy_fw1"], lp["hy_fb1"], lp["hy_fw2"], lp["hy_fb2"],
                                   lp["hy_fw3"], lp["hy_fb3"], lp["hy_sin_freq"], tc, ts)
        zc4 = _longconv(z4, kre, kim, tc[:, :SUB_LEN], ts[:, :SUB_LEN], ic, isn)
        yf = _ssd(xbc, dt, dtt, lp["ssm_dt_bias"], lp["ssm_A_log"], p["e_fwd"], reverse=False)
        ys = _ssd(xbc, dt, dtt, lp["ssm_dt_bias"], lp["ssm_A_log"], p["e_bwd"], reverse=True,
                  final_args=(yf, gate, lp["ssm_D_wide"], lp["ssm_norm"]))
        k, v = _kv_proj(mem, lp["norm_mem"], lp["w_kv"])
        x = _mix_attn(x, x0, z4, zc4, ys, lp["hy_bias"], lp["hy_norm"], lp["w_out"], lp["norm_xattn"],
                      lp["w_q"], k, v, lp["w_o"])
        x = _mlp(x, lp["norm_mlp"], lp["w_up"], lp["w_down"], p["norm_final"], final=(i == DEPTH - 1))
    return x


def kernel(x_prompt, x_sample, mem_prompt, mem_sample, norm_mix, w_in, hy_conv_w, hy_conv_b, hy_fw1, hy_fb1, hy_fw2, hy_fb2, hy_fw3, hy_fb3, hy_sin_freq, hy_bias, hy_norm, ssm_conv_w, ssm_conv_b, ssm_dt_bias, ssm_A_log, ssm_D, ssm_norm, w_out, norm_xattn, norm_mem, w_q, w_kv, w_o, norm_mlp, w_up, w_down, norm_final):
    row = lambda a: a.reshape(1, -1)
    layers = []
    for i in range(DEPTH):
        wi = w_in[i].astype(bf16)
        layers.append(dict(
            norm_mix=row(norm_mix[i]), w_main=wi[:, :MAIN_IN], w_dt=wi[:, MAIN_IN:], w_dtt=wi[:, MAIN_IN:].T,
            hy_conv_w=hy_conv_w[i], hy_conv_b=row(hy_conv_b[i]),
            ssm_conv_w=ssm_conv_w[i], ssm_conv_b=row(ssm_conv_b[i]),
            hy_fw1=hy_fw1[i], hy_fb1=hy_fb1[i], hy_fw2=hy_fw2[i], hy_fb2=hy_fb2[i], hy_fw3=hy_fw3[i],
            hy_fb3=hy_fb3[i], hy_sin_freq=hy_sin_freq[i], hy_bias=row(hy_bias[i]), hy_norm=row(hy_norm[i]),
            ssm_dt_bias=ssm_dt_bias[i], ssm_A_log=ssm_A_log[i],
            ssm_D_wide=row(jnp.repeat(ssm_D[i], SSM_HEAD_DIM)), ssm_norm=row(ssm_norm[i]),
            w_out=w_out[i].astype(bf16), norm_xattn=row(norm_xattn[i]), norm_mem=row(norm_mem[i]),
            w_q=w_q[i].astype(bf16), w_kv=w_kv[i].astype(bf16), w_o=w_o[i].astype(bf16),
            norm_mlp=row(norm_mlp[i]), w_up=w_up[i].astype(bf16), w_down=w_down[i].astype(bf16),
        ))
    lane_head = jnp.arange(SSM_WIDTH, dtype=jnp.int32)[None, :] // SSM_HEAD_DIM
    col = jnp.arange(DT_WIDTH, dtype=jnp.int32)[:, None]
    p = dict(layers=layers, norm_final=row(norm_final),
             e_fwd=(col == lane_head).astype(bf16), e_bwd=(col == lane_head + SSM_HEADS).astype(bf16))
    tables = _dft_tables()
    return (_trunk(x_prompt, mem_prompt, p, tables), _trunk(x_sample, mem_sample, p, tables))
```

```python
import functools
import math

import jax
import jax.numpy as jnp
from jax import lax
from jax.experimental import pallas as pl
from jax.experimental.pallas import tpu as pltpu

f32 = jnp.float32
bf16 = jnp.bfloat16

D_MODEL = 1024
DEPTH = 2
N_MEM = 256
HY_WIDTH = 512
HY_CONV = 3
HY_BANDS = 16
HY_ORDER = 64
HY_DECAY_TARGET = 1e-2
HY_FAST_PCT = 0.3
HY_SLOW_PCT = 1.5
SSM_WIDTH = 512
SSM_HEAD_DIM = 64
SSM_HEADS = 8
SSM_GROUPS = 2
SSM_STATE = 128
SSM_CONV = 5
SSM_CHUNK = 128
BC_WIDTH = SSM_GROUPS * SSM_STATE
XBC_WIDTH = SSM_WIDTH + 2 * BC_WIDTH
HY_IN = 3 * HY_WIDTH
MAIN_IN = HY_IN + SSM_WIDTH + XBC_WIDTH
DT_WIDTH = 2 * SSM_HEADS
XA_HEADS = 4
XA_HEAD_DIM = D_MODEL // XA_HEADS
D_FF = 4 * D_MODEL
EPS = 1e-5

LANES = 128
SUBLANES = 8
HALO = SUBLANES
SUB_LEN = 512
SUB_FFT = 2 * SUB_LEN
HY_SLABS = HY_WIDTH // LANES
ROW_TILE = 512
SSD_ROWS = 512
PW_ROWS = 64
VMEM_LIMIT = 56 * 1024 * 1024


def _cparams(*sem):
    return pltpu.CompilerParams(dimension_semantics=sem, vmem_limit_bytes=VMEM_LIMIT)


def _rms(x, w):
    return x * lax.rsqrt(jnp.mean(x * x, axis=-1, keepdims=True) + EPS) * w


def _dot(a, b):
    return jnp.dot(a, b, preferred_element_type=f32)


def _dot_hi(a, b):
    return jnp.dot(a, b, preferred_element_type=f32, precision=lax.Precision.HIGHEST)


def _dot_3pass(a, b):
    a_hi = a.astype(bf16)
    a_lo = (a - a_hi.astype(f32)).astype(bf16)
    b_hi = b.astype(bf16)
    b_lo = (b - b_hi.astype(f32)).astype(bf16)
    return _dot(a_hi, b_hi) + _dot(a_hi, b_lo) + _dot(a_lo, b_hi)


def _dot_nt(a, b):
    return lax.dot_general(a, b, (((1,), (1,)), ((), ())), preferred_element_type=f32)


def _dot_tn(a, b):
    return lax.dot_general(a, b, (((0,), (0,)), ((), ())), preferred_element_type=f32)


def _split3(x):
    hi = x.astype(bf16)
    r1 = x - hi.astype(f32)
    mid = r1.astype(bf16)
    lo = (r1 - mid.astype(f32)).astype(bf16)
    return hi, mid, lo


def _silu(x):
    return x / (1.0 + jnp.exp(-x))


def _softplus(x):
    return jnp.maximum(x, 0.0) + jnp.log(1.0 + jnp.exp(-jnp.abs(x)))


def _const_spec(shape):
    nd = len(shape)
    return pl.BlockSpec(shape, lambda *_: (0,) * nd)


def _inproj_kernel(xp_ref, x_ref, xn_ref, nw_ref, w_ref, wdtt_ref, hcw_ref, hcb_ref,
                   scw_ref, scb_ref, x0_ref, z_ref, gate_ref, xbc_ref, dtt_ref, pbuf):
    tm = x_ref.shape[1]
    i = pl.program_id(1)
    nw = nw_ref[...]
    xp = jnp.where(i > 0, _rms(xp_ref[0], nw), 0.0)
    xn = jnp.where(i < pl.num_programs(1) - 1, _rms(xn_ref[0], nw), 0.0)
    xm = _rms(x_ref[0], nw)
    xe = jnp.concatenate([xp, xm, xn], axis=0).astype(bf16)
    xmb = xe[HALO:HALO + tm]

    def conv_slabs(col0, cw_ref, cb_ref, taps, wcol0):
        p = _dot(xe, w_ref[:, col0:col0 + HY_WIDTH])
        for s in range(HY_SLABS):
            pbuf[s] = p[:, s * LANES:(s + 1) * LANES]
        pad = taps // 2
        outs = []
        for s in range(HY_SLABS):
            cs = slice(wcol0 + s * LANES, wcol0 + (s + 1) * LANES)
            acc = cb_ref[:, cs]
            for k in range(taps):
                acc = acc + pbuf[s, pl.ds(HALO + k - pad, tm), :] * cw_ref[k:k + 1, cs]
            outs.append(acc)
        return outs

    for s, blk in enumerate(conv_slabs(0, hcw_ref, hcb_ref, HY_CONV, 0)):
        x0_ref[0, :, s * LANES:(s + 1) * LANES] = blk
    for s, blk in enumerate(conv_slabs(HY_WIDTH, hcw_ref, hcb_ref, HY_CONV, HY_WIDTH)):
        z_ref[0, s] = blk
    for s, blk in enumerate(conv_slabs(2 * HY_WIDTH, hcw_ref, hcb_ref, HY_CONV, 2 * HY_WIDTH)):
        z_ref[0, s] = z_ref[0, s] * blk
    gate_ref[0] = _dot(xmb, w_ref[:, HY_IN:HY_IN + SSM_WIDTH])
    o2 = HY_IN + SSM_WIDTH
    for c in range(XBC_WIDTH // HY_WIDTH):
        for s, blk in enumerate(conv_slabs(o2 + c * HY_WIDTH, scw_ref, scb_ref, SSM_CONV, c * HY_WIDTH)):
            xbc_ref[0, :, c * HY_WIDTH + s * LANES:c * HY_WIDTH + (s + 1) * LANES] = _silu(blk)
    dtt_ref[0] = _dot_nt(wdtt_ref[...], xmb)


def _inproj(x, nw, w_main, w_dtt, hcw, hcb, scw, scb):
    b, l, d = x.shape
    tm = ROW_TILE
    nt = l // tm
    hb = tm // HALO
    grid = (b, nt)
    out_shape = (
        jax.ShapeDtypeStruct((b, l, HY_WIDTH), f32),
        jax.ShapeDtypeStruct((b, HY_SLABS, l, LANES), f32),
        jax.ShapeDtypeStruct((b, l, SSM_WIDTH), f32),
        jax.ShapeDtypeStruct((b, l, XBC_WIDTH), f32),
        jax.ShapeDtypeStruct((b, DT_WIDTH, l), f32),
    )
    in_specs = [
        pl.BlockSpec((1, HALO, d), lambda bi, i: (bi, jnp.maximum(i * hb - 1, 0), 0)),
        pl.BlockSpec((1, tm, d), lambda bi, i: (bi, i, 0)),
        pl.BlockSpec((1, HALO, d), lambda bi, i: (bi, jnp.minimum((i + 1) * hb, l // HALO - 1), 0)),
        _const_spec(nw.shape), _const_spec(w_main.shape),
        _const_spec(w_dtt.shape), _const_spec(hcw.shape), _const_spec(hcb.shape),
        _const_spec(scw.shape), _const_spec(scb.shape),
    ]
    out_specs = (
        pl.BlockSpec((1, tm, HY_WIDTH), lambda bi, i: (bi, i, 0)),
        pl.BlockSpec((1, HY_SLABS, tm, LANES), lambda bi, i: (bi, 0, i, 0)),
        pl.BlockSpec((1, tm, SSM_WIDTH), lambda bi, i: (bi, i, 0)),
        pl.BlockSpec((1, tm, XBC_WIDTH), lambda bi, i: (bi, i, 0)),
        pl.BlockSpec((1, DT_WIDTH, tm), lambda bi, i: (bi, 0, i)),
    )
    return pl.pallas_call(
        _inproj_kernel, grid=grid, in_specs=in_specs, out_specs=out_specs, out_shape=out_shape,
        scratch_shapes=[pltpu.VMEM((HY_SLABS, tm + 2 * HALO, LANES), f32)],
        compiler_params=_cparams("parallel", "parallel"), name="inproj",
    )(x, x, x, nw, w_main, w_dtt, hcw, hcb, scw, scb)


def _spectra_kernel(fw1t_ref, fw1c_ref, fw1s_ref, fb1_ref, fw2_ref, fb2_ref, fw3_ref, fb3_ref,
                    freq_ref, tc_ref, ts_ref, kre_ref, kim_ref, *, seq_len, phases):
    d = pl.program_id(0) - (phases - 1)
    lane = lax.broadcasted_iota(jnp.int32, (1, 2 * SUB_LEN), 1)
    m = lane % SUB_LEN
    neg = lane >= SUB_LEN
    j = jnp.where(neg, d - phases * m, d + phases * m)
    pos = jnp.abs(j).astype(f32)
    t = pos * (1.0 / (seq_len - 1))
    w = pos * (2.0 * math.pi / seq_len)
    band = lax.broadcasted_iota(jnp.int32, (HY_BANDS, 1), 0).astype(f32)
    fband = 1e-4 + band * ((HY_BANDS - 1 - 1e-4) / (HY_BANDS - 1))
    ang = fband * w
    fr = freq_ref[...]
    pre = fw1t_ref[...] * t + _dot_hi(fw1c_ref[...], jnp.cos(ang)) - _dot_hi(fw1s_ref[...], jnp.sin(ang))
    h = jnp.sin(fr * (pre + fb1_ref[...]))
    h = jnp.sin(fr * (_dot_hi(fw2_ref[...], h) + fb2_ref[...]))
    o = _dot_3pass(fw3_ref[...], h) + fb3_ref[...]
    sel = jnp.where(j >= 0, o[:HY_WIDTH], o[HY_WIDTH:])
    max_decay = math.log(HY_DECAY_TARGET) / HY_FAST_PCT
    min_decay = math.log(HY_DECAY_TARGET) / HY_SLOW_PCT
    ch = lax.broadcasted_iota(jnp.int32, (HY_WIDTH, 1), 0).astype(f32)
    deltas = jnp.abs(min_decay + ch * ((max_decay - min_decay) / (HY_WIDTH - 1)))
    g = sel * jnp.exp(-deltas * t)
    g = jnp.where(jnp.logical_and(neg, m == 0), 0.0, g)
    gp, gm = g[:, :SUB_LEN], g[:, SUB_LEN:]
    kre_ref[0] = _dot_nt(tc_ref[...], (gp + gm).astype(bf16))
    kim_ref[0] = _dot_nt(ts_ref[...], (gp - gm).astype(bf16))


def _filter_spectra(seq_len, fw1, fb1, fw2, fb2, fw3, fb3, freq, tc, ts):
    phases = seq_len // SUB_LEN
    nd = 2 * phases - 1
    col = lambda a: a.reshape(-1, 1)
    args = (col(fw1[0]), fw1[1:1 + HY_BANDS].T, fw1[1 + HY_BANDS:].T, col(fb1), fw2.T, col(fb2), fw3.T, col(fb3),
            col(freq), tc, ts)
    out_shape = (jax.ShapeDtypeStruct((nd, SUB_LEN, HY_WIDTH), f32),) * 2
    spec = pl.BlockSpec((1, SUB_LEN, HY_WIDTH), lambda k: (k, 0, 0))
    return pl.pallas_call(
        functools.partial(_spectra_kernel, seq_len=seq_len, phases=phases),
        grid=(nd,), in_specs=[_const_spec(a.shape) for a in args], out_specs=(spec, spec),
        out_shape=out_shape, compiler_params=_cparams("parallel"), name="filter_spectra",
    )(*args)


def _dft_tables():
    f = jnp.arange(SUB_LEN, dtype=jnp.int32)[:, None]
    n = jnp.arange(SUB_LEN, dtype=jnp.int32)[None, :]
    ang = (((2 * f + 1) * n) % (2 * SUB_FFT)).astype(f32) * (math.pi / SUB_FFT)
    c, s = jnp.cos(ang), jnp.sin(ang)
    tc, ts = c.astype(bf16), (-s).astype(bf16)
    scale = 2.0 / SUB_FFT
    ic = (scale * c).T.astype(bf16)
    isn = (-scale * s).T.astype(bf16)
    return tc, ts, ic, isn


def _longconv_kernel(z_ref, kre_ref, kim_ref, tc_ref, ts_ref, ic_ref, is_ref, o_ref,
                     zre, zim, wre, wim, *, phases):
    slabs = z_ref.shape[1]
    for q in range(phases):
        zq = jnp.concatenate(
            [z_ref[0, s, pl.ds(q, SUB_LEN, stride=phases), :] for s in range(slabs)], axis=1).astype(bf16)
        zre[q] = _dot(tc_ref[...], zq)
        zim[q] = _dot(ts_ref[...], zq)
    for p in range(phases):
        def rows(r, carry, p=p):
            rs = pl.ds(pl.multiple_of(r * PW_ROWS, PW_ROWS), PW_ROWS)
            are = jnp.zeros((PW_ROWS, zre.shape[2]), f32)
            aim = jnp.zeros((PW_ROWS, zre.shape[2]), f32)
            for q in range(phases):
                kd = p - q + phases - 1
                kr, ki = kre_ref[kd, rs, :], kim_ref[kd, rs, :]
                xr, xi = zre[q, rs, :], zim[q, rs, :]
                are = are + (kr * xr - ki * xi)
                aim = aim + (kr * xi + ki * xr)
            wre[rs, :] = are.astype(bf16)
            wim[rs, :] = aim.astype(bf16)
            return carry
        lax.fori_loop(0, SUB_LEN // PW_ROWS, rows, 0)
        yp = _dot(ic_ref[...], wre[...]) + _dot(is_ref[...], wim[...])
        for s in range(slabs):
            o_ref[0, s, pl.ds(p, SUB_LEN, stride=phases), :] = yp[:, s * LANES:(s + 1) * LANES]


def _longconv(z4, kre, kim, tcf, tsf, ic, isn):
    b, slabs, l, _ = z4.shape
    phases = l // SUB_LEN
    nd = 2 * phases - 1
    sb = 2
    cb = sb * LANES
    zspec = pl.BlockSpec((1, sb, l, LANES), lambda j, bi: (bi, j, 0, 0))
    kspec = pl.BlockSpec((nd, SUB_LEN, cb), lambda j, bi: (0, 0, j), pipeline_mode=pl.Buffered(1))
    tspec = _const_spec(tcf.shape)
    return pl.pallas_call(
        functools.partial(_longconv_kernel, phases=phases),
        grid=(slabs // sb, b),
        in_specs=[zspec, kspec, kspec, tspec, tspec, tspec, tspec], out_specs=zspec,
        out_shape=jax.ShapeDtypeStruct(z4.shape, f32),
        scratch_shapes=[pltpu.VMEM((phases, SUB_LEN, cb), f32), pltpu.VMEM((phases, SUB_LEN, cb), f32),
                        pltpu.VMEM((SUB_LEN, cb), bf16), pltpu.VMEM((SUB_LEN, cb), bf16)],
        compiler_params=_cparams("parallel", "parallel"), name="longconv",
    )(z4, kre, kim, tcf, tsf, ic, isn)


def _ssd_kernel(*refs, reverse, final):
    if final:
        xbc_ref, dtt_ref, dtb_ref, al_ref, ex_ref, yf_ref, gate_ref, dskip_ref, nw_ref, y_ref, h_ref = refs
    else:
        xbc_ref, dtt_ref, dtb_ref, al_ref, ex_ref, y_ref, h_ref = refs
    q = SSM_CHUNK
    nch = xbc_ref.shape[1] // q
    nh = SSM_HEADS
    col0 = nh if reverse else 0
    log2e = 1.0 / math.log(2.0)

    @pl.when(pl.program_id(1) == 0)
    def _():
        h_ref[...] = jnp.zeros_like(h_ref)

    ii = lax.broadcasted_iota(jnp.int32, (q, q), 0)
    jj = lax.broadcasted_iota(jnp.int32, (q, q), 1)
    mask = (ii <= jj) if reverse else (ii >= jj)
    m_row = jnp.where((ii >= jj) if reverse else (ii <= jj), 1.0, 0.0).astype(bf16)
    hp = (nh // SSM_GROUPS) * SSM_HEAD_DIM
    first_head = lax.broadcasted_iota(jnp.int32, (q, LANES), 1) < SSM_HEAD_DIM

    dt_all = _softplus(dtt_ref[0, col0:col0 + nh, :] + dtb_ref[...])
    da_all = dt_all * (-jnp.exp(al_ref[...]))
    stack = lambda a: jnp.concatenate([a[:, c * q:(c + 1) * q] for c in range(nch)], axis=0)
    dt = stack(dt_all)
    cs2 = sum(_dot(part, m_row) for part in _split3(stack(da_all))) * log2e
    edge = cs2[:, 0:1] if reverse else cs2[:, q - 1:q]
    e = jnp.exp2(cs2)
    dd = dt * jnp.exp2(edge - cs2)
    rt = cs2 - jnp.log2(dt)
    e_hi = e.astype(bf16).astype(f32)
    dd_hi = dd.astype(bf16).astype(f32)
    pad = jnp.zeros((q - 5 * nh, q), f32)

    h = h_ref[...]
    for ci in range(nch):
        c = (nch - 1 - ci) if reverse else ci
        rows = slice(c * q, (c + 1) * q)
        hr = slice(c * nh, (c + 1) * nh)
        xs = xbc_ref[0, rows, 0:SSM_WIDTH]
        bm = xbc_ref[0, rows, SSM_WIDTH:SSM_WIDTH + BC_WIDTH].astype(bf16)
        cm = xbc_ref[0, rows, SSM_WIDTH + BC_WIDTH:XBC_WIDTH].astype(bf16)
        cols = jnp.concatenate([cs2[hr], e_hi[hr], (e - e_hi)[hr], dd_hi[hr], (dd - dd_hi)[hr], pad], axis=0).T
        wide = _dot(cols.astype(bf16), ex_ref[...])
        e_x, dd_x = wide[:, :SSM_WIDTH], wide[:, SSM_WIDTH:]
        xsb = xs.astype(bf16)
        xdd = (xs * dd_x).astype(bf16)
        hb = h.astype(bf16)
        ys, sts = [], []
        for g in range(SSM_GROUPS):
            bg = bm[:, g * SSM_STATE:(g + 1) * SSM_STATE]
            cg = cm[:, g * SSM_STATE:(g + 1) * SSM_STATE]
            gl = slice(g * hp, (g + 1) * hp)
            cb = _dot_nt(cg, bg)
            yoff = _dot(cg, hb[:, gl]) * e_x[:, gl]
            sts.append(_dot_tn(bg, xdd[:, gl]))
            yd = []
            for pr in range(nh // SSM_GROUPS // 2):
                ms = []
                for hd in (g * (nh // SSM_GROUPS) + 2 * pr, g * (nh // SSM_GROUPS) + 2 * pr + 1):
                    diff = cols[:, hd:hd + 1] - rt[c * nh + hd:c * nh + hd + 1, :]
                    ms.append((cb * jnp.exp2(jnp.where(mask, diff, -1e30))).astype(bf16))
                xp = xsb[:, g * hp + pr * LANES:g * hp + (pr + 1) * LANES]
                zero = jnp.zeros_like(xp)
                rhs = jnp.concatenate([jnp.where(first_head, xp, zero), jnp.where(first_head, zero, xp)], axis=0)
                yd.append(_dot(jnp.concatenate(ms, axis=1), rhs))
            ys.append(jnp.concatenate(yd, axis=1) + yoff)
        y = jnp.concatenate(ys, axis=1)
        edge_x = e_x[0:1, :] if reverse else e_x[q - 1:q, :]
        h = h * edge_x + jnp.concatenate(sts, axis=1)
        if final:
            tot = yf_ref[0, rows, :] + y + xs * dskip_ref[...]
            gated = tot * _silu(gate_ref[0, rows, :])
            y_ref[0, rows, :] = _rms(gated, nw_ref[...]).astype(y_ref.dtype)
        else:
            y_ref[0, rows, :] = y
    h_ref[...] = h


def _ssd(xbc, dtt, dtb, alog, expand, *, reverse, final_args=None):
    b, l, _ = xbc.shape
    tr = SSD_ROWS
    nt = l // tr
    final = final_args is not None
    d = 1 if reverse else 0

    def blk(i):
        return (nt - 1 - i) if reverse else i

    row3 = lambda w: pl.BlockSpec((1, tr, w), lambda bi, i: (bi, blk(i), 0))
    small = (dtb[d].reshape(SSM_HEADS, 1), alog[d].reshape(SSM_HEADS, 1), expand)
    args = [xbc, dtt, *small]
    in_specs = [row3(XBC_WIDTH), pl.BlockSpec((1, DT_WIDTH, tr), lambda bi, i: (bi, 0, blk(i))),
                *[_const_spec(a.shape) for a in small]]
    if final:
        yf, gate, dskip, nw = final_args
        args += [yf, gate, dskip, nw]
        in_specs += [row3(SSM_WIDTH), row3(SSM_WIDTH), _const_spec(dskip.shape), _const_spec(nw.shape)]
    return pl.pallas_call(
        functools.partial(_ssd_kernel, reverse=reverse, final=final),
        grid=(b, nt), in_specs=in_specs, out_specs=row3(SSM_WIDTH),
        out_shape=jax.ShapeDtypeStruct((b, l, SSM_WIDTH), bf16 if final else f32),
        scratch_shapes=[pltpu.VMEM((SSM_STATE, SSM_WIDTH), f32)],
        compiler_params=_cparams("parallel", "arbitrary"), name="ssd_bwd" if reverse else "ssd_fwd",
    )(*args)


def _ssd_expand_matrix():
    row = jnp.arange(SSM_CHUNK, dtype=jnp.int32)[:, None]
    lane = jnp.arange(2 * SSM_WIDTH, dtype=jnp.int32)[None, :]
    head = (lane % SSM_WIDTH) // SSM_HEAD_DIM
    quantity = lane // SSM_WIDTH
    base = SSM_HEADS + 2 * SSM_HEADS * quantity
    hit = jnp.logical_or(row == base + head, row == base + SSM_HEADS + head)
    return hit.astype(bf16)


def _kv_kernel(mem_ref, nw_ref, wkv_ref, k_ref, v_ref):
    mn = _rms(mem_ref[0], nw_ref[...]).astype(bf16)
    k_ref[0] = _dot(mn, wkv_ref[:, :D_MODEL]).astype(bf16)
    v_ref[0] = _dot(mn, wkv_ref[:, D_MODEL:]).astype(bf16)


def _kv_proj(mem, nw, wkv):
    b, m, d = mem.shape
    spec = pl.BlockSpec((1, m, d), lambda bi: (bi, 0, 0))
    return pl.pallas_call(
        _kv_kernel, grid=(b,), in_specs=[spec, _const_spec(nw.shape), _const_spec(wkv.shape)],
        out_specs=(spec, spec), out_shape=(jax.ShapeDtypeStruct((b, m, d), bf16),) * 2,
        compiler_params=_cparams("parallel"), name="kv_proj",
    )(mem, nw, wkv)


def _mix_attn_kernel(x_ref, x0_ref, z_ref, zc_ref, ys_ref, hb_ref, hn_ref, wout_ref, nx_ref, wq_ref,
                     k_ref, v_ref, wo_ref, o_ref):
    z = jnp.concatenate([z_ref[0, s] for s in range(HY_SLABS)], axis=1)
    zc = jnp.concatenate([zc_ref[0, s] for s in range(HY_SLABS)], axis=1)
    y_hy = _rms(x0_ref[0] * (zc + z * hb_ref[...]), hn_ref[...]).astype(bf16)
    x1 = x_ref[0] + _dot(y_hy, wout_ref[:HY_WIDTH, :]) + _dot(ys_ref[0], wout_ref[HY_WIDTH:, :])
    xn = _rms(x1, nx_ref[...]).astype(bf16)
    qa = (_dot(xn, wq_ref[...]) * (XA_HEAD_DIM ** -0.5)).astype(bf16)
    heads = []
    for h in range(XA_HEADS):
        hs = slice(h * XA_HEAD_DIM, (h + 1) * XA_HEAD_DIM)
        s = _dot_nt(qa[:, hs], k_ref[0, :, hs])
        e = jnp.exp(s - jnp.max(s, axis=-1, keepdims=True))
        den = jnp.sum(e, axis=-1, keepdims=True)
        heads.append(_dot(e.astype(bf16), v_ref[0, :, hs]) / den)
    o = jnp.concatenate(heads, axis=1).astype(bf16)
    o_ref[0] = x1 + _dot(o, wo_ref[...])


def _mix_attn(x, x0, z4, zc4, ys, hbias, hnorm, wout, nx, wq, k, v, wo):
    b, l, d = x.shape
    tm = ROW_TILE
    row = lambda w: pl.BlockSpec((1, tm, w), lambda bi, i: (bi, i, 0))
    slab = pl.BlockSpec((1, HY_SLABS, tm, LANES), lambda bi, i: (bi, 0, i, 0))
    memspec = pl.BlockSpec((1, N_MEM, d), lambda bi, i: (bi, 0, 0))
    consts = (hbias, hnorm, wout, nx, wq)
    return pl.pallas_call(
        _mix_attn_kernel, grid=(b, l // tm),
        in_specs=[row(d), row(HY_WIDTH), slab, slab, row(SSM_WIDTH), *[_const_spec(a.shape) for a in consts],
                  memspec, memspec, _const_spec(wo.shape)],
        out_specs=row(d), out_shape=jax.ShapeDtypeStruct((b, l, d), f32),
        compiler_params=_cparams("parallel", "parallel"), name="mix_attn",
    )(x, x0, z4, zc4, ys, *consts, k, v, wo)


def _mlp_kernel(x_ref, nw_ref, wup_ref, wdn_ref, fw_ref, o_ref, *, final):
    x = x_ref[0]
    xn = _rms(x, nw_ref[...]).astype(bf16)
    acc = x
    for c in range(D_FF // D_MODEL):
        cs = slice(c * D_MODEL, (c + 1) * D_MODEL)
        hdn = jnp.maximum(_dot(xn, wup_ref[:, cs]), 0.0)
        acc = acc + _dot((hdn * hdn).astype(bf16), wdn_ref[cs, :])
    o_ref[0] = _rms(acc, fw_ref[...]) if final else acc


def _mlp(x, nw, wup, wdn, fw, final):
    b, l, d = x.shape
    tm = ROW_TILE
    row = pl.BlockSpec((1, tm, d), lambda bi, i: (bi, i, 0))
    wspec = lambda a: pl.BlockSpec(a.shape, lambda bi, i: (0, 0), pipeline_mode=pl.Buffered(1))
    return pl.pallas_call(
        functools.partial(_mlp_kernel, final=final), grid=(b, l // tm),
        in_specs=[row, _const_spec(nw.shape), wspec(wup), wspec(wdn), _const_spec(fw.shape)],
        out_specs=row, out_shape=jax.ShapeDtypeStruct((b, l, d), f32),
        compiler_params=_cparams("parallel", "parallel"), name="mlp",
    )(x, nw, wup, wdn, fw)


def _trunk(x, mem, p, tables):
    tc, ts, ic, isn = tables
    seq_len = x.shape[1]
    for i in range(DEPTH):
        lp = p["layers"][i]
        x0, z4, gate, xbc, dtt = _inproj(x, lp["norm_mix"], lp["w_main"], lp["w_dtt"],
                                         lp["hy_conv_w"], lp["hy_conv_b"], lp["ssm_conv_w"], lp["ssm_conv_b"])
        kre, kim = _filter_spectra(seq_len, lp["hy_fw1"], lp["hy_fb1"], lp["hy_fw2"], lp["hy_fb2"],
                                   lp["hy_fw3"], lp["hy_fb3"], lp["hy_sin_freq"], tc, ts)
        zc4 = _longconv(z4, kre, kim, tc, ts, ic, isn)
        yf = _ssd(xbc, dtt, lp["ssm_dt_bias"], lp["ssm_A_log"], p["ssd_expand"], reverse=False)
        ys = _ssd(xbc, dtt, lp["ssm_dt_bias"], lp["ssm_A_log"], p["ssd_expand"], reverse=True,
                  final_args=(yf, gate, lp["ssm_D_wide"], lp["ssm_norm"]))
        k, v = _kv_proj(mem, lp["norm_mem"], lp["w_kv"])
        x = _mix_attn(x, x0, z4, zc4, ys, lp["hy_bias"], lp["hy_norm"], lp["w_out"], lp["norm_xattn"],
                      lp["w_q"], k, v, lp["w_o"])
        x = _mlp(x, lp["norm_mlp"], lp["w_up"], lp["w_down"], p["norm_final"], final=(i == DEPTH - 1))
    return x


def kernel(x_prompt, x_sample, mem_prompt, mem_sample, norm_mix, w_in, hy_conv_w, hy_conv_b, hy_fw1, hy_fb1, hy_fw2, hy_fb2, hy_fw3, hy_fb3, hy_sin_freq, hy_bias, hy_norm, ssm_conv_w, ssm_conv_b, ssm_dt_bias, ssm_A_log, ssm_D, ssm_norm, w_out, norm_xattn, norm_mem, w_q, w_kv, w_o, norm_mlp, w_up, w_down, norm_final):
    row = lambda a: a.reshape(1, -1)
    layers = []
    for i in range(DEPTH):
        wi = w_in[i].astype(bf16)
        layers.append(dict(
            norm_mix=row(norm_mix[i]), w_main=wi[:, :MAIN_IN], w_dtt=wi[:, MAIN_IN:].T,
            hy_conv_w=hy_conv_w[i], hy_conv_b=row(hy_conv_b[i]),
            ssm_conv_w=ssm_conv_w[i], ssm_conv_b=row(ssm_conv_b[i]),
            hy_fw1=hy_fw1[i], hy_fb1=hy_fb1[i], hy_fw2=hy_fw2[i], hy_fb2=hy_fb2[i], hy_fw3=hy_fw3[i],
            hy_fb3=hy_fb3[i], hy_sin_freq=hy_sin_freq[i], hy_bias=row(hy_bias[i]), hy_norm=row(hy_norm[i]),
            ssm_dt_bias=ssm_dt_bias[i], ssm_A_log=ssm_A_log[i],
            ssm_D_wide=row(jnp.repeat(ssm_D[i], SSM_HEAD_DIM)), ssm_norm=row(ssm_norm[i]),
            w_out=w_out[i].astype(bf16), norm_xattn=row(norm_xattn[i]), norm_mem=row(norm_mem[i]),
            w_q=w_q[i].astype(bf16), w_kv=w_kv[i].astype(bf16), w_o=w_o[i].astype(bf16),
            norm_mlp=row(norm_mlp[i]), w_up=w_up[i].astype(bf16), w_down=w_down[i].astype(bf16),
        ))
    p = dict(layers=layers, norm_final=row(norm_final), ssd_expand=_ssd_expand_matrix())
    tables = _dft_tables()
    return (_trunk(x_prompt, mem_prompt, p, tables), _trunk(x_sample, mem_sample, p, tables))
```

```python
import functools
import math

import jax
import jax.numpy as jnp
from jax import lax
from jax.experimental import pallas as pl
from jax.experimental.pallas import tpu as pltpu

f32 = jnp.float32
bf16 = jnp.bfloat16

D_MODEL = 1024
DEPTH = 2
N_MEM = 256
HY_WIDTH = 512
HY_CONV = 3
HY_BANDS = 16
HY_ORDER = 64
HY_DECAY_TARGET = 1e-2
HY_FAST_PCT = 0.3
HY_SLOW_PCT = 1.5
SSM_WIDTH = 512
SSM_HEAD_DIM = 64
SSM_HEADS = 8
SSM_GROUPS = 2
SSM_STATE = 128
SSM_CONV = 5
SSM_CHUNK = 128
BC_WIDTH = SSM_GROUPS * SSM_STATE
XBC_WIDTH = SSM_WIDTH + 2 * BC_WIDTH
HY_IN = 3 * HY_WIDTH
MAIN_IN = HY_IN + SSM_WIDTH + XBC_WIDTH
DT_WIDTH = 2 * SSM_HEADS
XA_HEADS = 4
XA_HEAD_DIM = D_MODEL // XA_HEADS
D_FF = 4 * D_MODEL
EPS = 1e-5

LANES = 128
SUBLANES = 8
HALO = SUBLANES
SUB_LEN = 512
SUB_FFT = 2 * SUB_LEN
HY_SLABS = HY_WIDTH // LANES
ROW_TILE = 512
SSD_ROWS = 1024
PW_ROWS = 32
VMEM_LIMIT = 56 * 1024 * 1024


def _cparams(*sem):
    return pltpu.CompilerParams(dimension_semantics=sem, vmem_limit_bytes=VMEM_LIMIT)


def _rms(x, w):
    return x * lax.rsqrt(jnp.mean(x * x, axis=-1, keepdims=True) + EPS) * w


def _dot(a, b):
    return jnp.dot(a, b, preferred_element_type=f32)


def _dot_hi(a, b):
    return jnp.dot(a, b, preferred_element_type=f32, precision=lax.Precision.HIGHEST)


def _dot_3pass(a, b):
    a_hi = a.astype(bf16)
    a_lo = (a - a_hi.astype(f32)).astype(bf16)
    b_hi = b.astype(bf16)
    b_lo = (b - b_hi.astype(f32)).astype(bf16)
    return _dot(a_hi, b_hi) + _dot(a_hi, b_lo) + _dot(a_lo, b_hi)


def _dot_nt(a, b):
    return lax.dot_general(a, b, (((1,), (1,)), ((), ())), preferred_element_type=f32)


def _dot_tn(a, b):
    return lax.dot_general(a, b, (((0,), (0,)), ((), ())), preferred_element_type=f32)


def _split3(x):
    hi = x.astype(bf16)
    r1 = x - hi.astype(f32)
    mid = r1.astype(bf16)
    lo = (r1 - mid.astype(f32)).astype(bf16)
    return hi, mid, lo


def _silu(x):
    return x / (1.0 + jnp.exp(-x))


def _softplus(x):
    return jnp.maximum(x, 0.0) + jnp.log(1.0 + jnp.exp(-jnp.abs(x)))


def _const_spec(shape):
    nd = len(shape)
    return pl.BlockSpec(shape, lambda *_: (0,) * nd)


def _inproj_kernel(xp_ref, x_ref, xn_ref, nw_ref, w_ref, wdtt_ref, hcw_ref, hcb_ref,
                   scw_ref, scb_ref, x0_ref, z_ref, gate_ref, xbc_ref, dtt_ref, pbuf):
    tm = x_ref.shape[1]
    i = pl.program_id(1)
    nw = nw_ref[...]
    xp = jnp.where(i > 0, _rms(xp_ref[0], nw), 0.0)
    xn = jnp.where(i < pl.num_programs(1) - 1, _rms(xn_ref[0], nw), 0.0)
    xm = _rms(x_ref[0], nw)
    xe = jnp.concatenate([xp, xm, xn], axis=0).astype(bf16)
    xmb = xe[HALO:HALO + tm]

    def conv_slabs(buf, col0, cw_ref, cb_ref, taps, wcol0):
        p = _dot(xe, w_ref[:, col0:col0 + HY_WIDTH])
        for s in range(HY_SLABS):
            pbuf[buf, s] = p[:, s * LANES:(s + 1) * LANES]
        pad = taps // 2
        outs = []
        for s in range(HY_SLABS):
            cs = slice(wcol0 + s * LANES, wcol0 + (s + 1) * LANES)
            acc = cb_ref[:, cs]
            for k in range(taps):
                acc = acc + pbuf[buf, s, pl.ds(HALO + k - pad, tm), :] * cw_ref[k:k + 1, cs]
            outs.append(acc)
        return outs

    for s, blk in enumerate(conv_slabs(0, 0, hcw_ref, hcb_ref, HY_CONV, 0)):
        x0_ref[0, :, s * LANES:(s + 1) * LANES] = blk
    for s, blk in enumerate(conv_slabs(1, HY_WIDTH, hcw_ref, hcb_ref, HY_CONV, HY_WIDTH)):
        z_ref[0, s] = blk
    for s, blk in enumerate(conv_slabs(0, 2 * HY_WIDTH, hcw_ref, hcb_ref, HY_CONV, 2 * HY_WIDTH)):
        z_ref[0, s] = z_ref[0, s] * blk
    gate_ref[0] = _dot(xmb, w_ref[:, HY_IN:HY_IN + SSM_WIDTH])
    o2 = HY_IN + SSM_WIDTH
    for c in range(XBC_WIDTH // HY_WIDTH):
        for s, blk in enumerate(conv_slabs(1 - c % 2, o2 + c * HY_WIDTH, scw_ref, scb_ref, SSM_CONV, c * HY_WIDTH)):
            xbc_ref[0, :, c * HY_WIDTH + s * LANES:c * HY_WIDTH + (s + 1) * LANES] = _silu(blk)
    dtt_ref[0] = _dot_nt(wdtt_ref[...], xmb)


def _inproj(x, nw, w_main, w_dtt, hcw, hcb, scw, scb):
    b, l, d = x.shape
    tm = ROW_TILE
    nt = l // tm
    hb = tm // HALO
    grid = (b, nt)
    out_shape = (
        jax.ShapeDtypeStruct((b, l, HY_WIDTH), f32),
        jax.ShapeDtypeStruct((b, HY_SLABS, l, LANES), f32),
        jax.ShapeDtypeStruct((b, l, SSM_WIDTH), f32),
        jax.ShapeDtypeStruct((b, l, XBC_WIDTH), f32),
        jax.ShapeDtypeStruct((b, DT_WIDTH, l), f32),
    )
    in_specs = [
        pl.BlockSpec((1, HALO, d), lambda bi, i: (bi, jnp.maximum(i * hb - 1, 0), 0)),
        pl.BlockSpec((1, tm, d), lambda bi, i: (bi, i, 0)),
        pl.BlockSpec((1, HALO, d), lambda bi, i: (bi, jnp.minimum((i + 1) * hb, l // HALO - 1), 0)),
        _const_spec(nw.shape), _const_spec(w_main.shape),
        _const_spec(w_dtt.shape), _const_spec(hcw.shape), _const_spec(hcb.shape),
        _const_spec(scw.shape), _const_spec(scb.shape),
    ]
    out_specs = (
        pl.BlockSpec((1, tm, HY_WIDTH), lambda bi, i: (bi, i, 0)),
        pl.BlockSpec((1, HY_SLABS, tm, LANES), lambda bi, i: (bi, 0, i, 0)),
        pl.BlockSpec((1, tm, SSM_WIDTH), lambda bi, i: (bi, i, 0)),
        pl.BlockSpec((1, tm, XBC_WIDTH), lambda bi, i: (bi, i, 0)),
        pl.BlockSpec((1, DT_WIDTH, tm), lambda bi, i: (bi, 0, i)),
    )
    return pl.pallas_call(
        _inproj_kernel, grid=grid, in_specs=in_specs, out_specs=out_specs, out_shape=out_shape,
        scratch_shapes=[pltpu.VMEM((2, HY_SLABS, tm + 2 * HALO, LANES), f32)],
        compiler_params=_cparams("parallel", "parallel"), name="inproj",
    )(x, x, x, nw, w_main, w_dtt, hcw, hcb, scw, scb)


def _spectra_kernel(fw1t_ref, fw1c_ref, fw1s_ref, fb1_ref, fw2_ref, fb2_ref, fw3_ref, fb3_ref,
                    freq_ref, tc_ref, ts_ref, kre_ref, kim_ref, *, seq_len, phases):
    d = pl.program_id(0)
    lane = lax.broadcasted_iota(jnp.int32, (1, 2 * SUB_LEN), 1)
    m = lane % SUB_LEN
    pos = jnp.abs(jnp.where(lane >= SUB_LEN, d - phases * m, d + phases * m)).astype(f32)
    t = pos * (1.0 / (seq_len - 1))
    w = pos * (2.0 * math.pi / seq_len)
    band = lax.broadcasted_iota(jnp.int32, (HY_BANDS, 1), 0).astype(f32)
    fband = 1e-4 + band * ((HY_BANDS - 1 - 1e-4) / (HY_BANDS - 1))
    ang = fband * w
    fr = freq_ref[...]
    pre = fw1t_ref[...] * t + _dot_hi(fw1c_ref[...], jnp.cos(ang)) - _dot_hi(fw1s_ref[...], jnp.sin(ang))
    h = jnp.sin(fr * (pre + fb1_ref[...]))
    h = jnp.sin(fr * (_dot_hi(fw2_ref[...], h) + fb2_ref[...]))
    o = _dot_3pass(fw3_ref[...], h) + fb3_ref[...]
    max_decay = math.log(HY_DECAY_TARGET) / HY_FAST_PCT
    min_decay = math.log(HY_DECAY_TARGET) / HY_SLOW_PCT
    ch = lax.broadcasted_iota(jnp.int32, (HY_WIDTH, 1), 0).astype(f32)
    deltas = jnp.abs(min_decay + ch * ((max_decay - min_decay) / (HY_WIDTH - 1)))
    decay = jnp.exp(-deltas * t)
    fwd = o[:HY_WIDTH] * decay
    bwd = o[HY_WIDTH:] * decay
    lag0 = m[:, :SUB_LEN] == 0
    gp = fwd[:, :SUB_LEN]
    gm = jnp.where(lag0, 0.0, bwd[:, SUB_LEN:])
    kre_ref[0, 0] = _dot_nt(tc_ref[...], (gp + gm).astype(bf16))
    kim_ref[0, 0] = _dot_nt(ts_ref[...], (gp - gm).astype(bf16))
    gp = jnp.where(jnp.logical_and(lag0, d > 0), bwd[:, SUB_LEN:], fwd[:, SUB_LEN:])
    gm = jnp.where(lag0, 0.0, bwd[:, :SUB_LEN])
    kre_ref[1, 0] = _dot_nt(tc_ref[...], (gp + gm).astype(bf16))
    kim_ref[1, 0] = _dot_nt(ts_ref[...], (gp - gm).astype(bf16))


def _filter_spectra(seq_len, fw1, fb1, fw2, fb2, fw3, fb3, freq, tc, ts):
    phases = seq_len // SUB_LEN
    col = lambda a: a.reshape(-1, 1)
    args = (col(fw1[0]), fw1[1:1 + HY_BANDS].T, fw1[1 + HY_BANDS:].T, col(fb1), fw2.T, col(fb2), fw3.T, col(fb3),
            col(freq), tc, ts)
    out_shape = (jax.ShapeDtypeStruct((2, phases, SUB_LEN, HY_WIDTH), f32),) * 2
    spec = pl.BlockSpec((2, 1, SUB_LEN, HY_WIDTH), lambda k: (0, k, 0, 0))
    return pl.pallas_call(
        functools.partial(_spectra_kernel, seq_len=seq_len, phases=phases),
        grid=(phases,), in_specs=[_const_spec(a.shape) for a in args], out_specs=(spec, spec),
        out_shape=out_shape, compiler_params=_cparams("parallel"), name="filter_spectra",
    )(*args)


def _dft_tables():
    f = jnp.arange(SUB_LEN, dtype=jnp.int32)[:, None]
    n = jnp.arange(SUB_LEN, dtype=jnp.int32)[None, :]
    ang = (((2 * f + 1) * n) % (2 * SUB_FFT)).astype(f32) * (math.pi / SUB_FFT)
    c, s = jnp.cos(ang), jnp.sin(ang)
    tc, ts = c.astype(bf16), (-s).astype(bf16)
    scale = 2.0 / SUB_FFT
    ic = (scale * c).T.astype(bf16)
    isn = (-scale * s).T.astype(bf16)
    return tc, ts, ic, isn


def _longconv_kernel(z_ref, kre_ref, kim_ref, tc_ref, ts_ref, ic_ref, is_ref, o_ref,
                     zre, zim, wre, wim, *, phases):
    slabs = z_ref.shape[1]
    for q in range(phases):
        zq = jnp.concatenate(
            [z_ref[0, s, pl.ds(q, SUB_LEN, stride=phases), :] for s in range(slabs)], axis=1).astype(bf16)
        zre[q] = _dot(tc_ref[...], zq)
        zim[q] = _dot(ts_ref[...], zq)

    def pointwise(p, slot):
        for r in range(SUB_LEN // PW_ROWS):
            rs = slice(r * PW_ROWS, (r + 1) * PW_ROWS)
            are = jnp.zeros((PW_ROWS, zre.shape[2]), f32)
            aim = jnp.zeros((PW_ROWS, zre.shape[2]), f32)
            for q in range(phases):
                kd = jnp.where(p >= q, p - q, phases + q - p)
                kr, ki = kre_ref[kd, rs, :], kim_ref[kd, rs, :]
                xr, xi = zre[q, rs, :], zim[q, rs, :]
                are = are + (kr * xr - ki * xi)
                aim = aim + (kr * xi + ki * xr)
            wre[slot, rs, :] = are.astype(bf16)
            wim[slot, rs, :] = aim.astype(bf16)

    def inverse(p, slot):
        yp = _dot(ic_ref[...], wre[slot]) + _dot(is_ref[...], wim[slot])
        for s in range(slabs):
            o_ref[0, s, pl.ds(p, SUB_LEN, stride=phases), :] = yp[:, s * LANES:(s + 1) * LANES]

    pointwise(0, 0)

    def body(p, carry):
        pointwise(p, p % 2)
        inverse(p - 1, (p - 1) % 2)
        return carry

    lax.fori_loop(1, phases, body, 0)
    inverse(phases - 1, (phases - 1) % 2)


def _longconv(z4, kre, kim, tcf, tsf, ic, isn):
    b, slabs, l, _ = z4.shape
    phases = l // SUB_LEN
    sb = 2
    cb = sb * LANES
    zspec = pl.BlockSpec((1, sb, l, LANES), lambda j, bi: (bi, j, 0, 0))
    kspec = pl.BlockSpec((2 * phases, SUB_LEN, cb), lambda j, bi: (0, 0, j), pipeline_mode=pl.Buffered(1))
    tspec = _const_spec(tcf.shape)
    return pl.pallas_call(
        functools.partial(_longconv_kernel, phases=phases),
        grid=(slabs // sb, b),
        in_specs=[zspec, kspec, kspec, tspec, tspec, tspec, tspec], out_specs=zspec,
        out_shape=jax.ShapeDtypeStruct(z4.shape, f32),
        scratch_shapes=[pltpu.VMEM((phases, SUB_LEN, cb), f32), pltpu.VMEM((phases, SUB_LEN, cb), f32),
                        pltpu.VMEM((2, SUB_LEN, cb), bf16), pltpu.VMEM((2, SUB_LEN, cb), bf16)],
        compiler_params=_cparams("parallel", "parallel"), name="longconv",
    )(z4, kre.reshape(2 * phases, SUB_LEN, HY_WIDTH), kim.reshape(2 * phases, SUB_LEN, HY_WIDTH),
      tcf, tsf, ic, isn)


def _ssd_kernel(*refs, reverse, final):
    if final:
        xbc_ref, dtt_ref, dtb_ref, al_ref, ex_ref, yf_ref, gate_ref, dskip_ref, nw_ref, y_ref, h_ref = refs
    else:
        xbc_ref, dtt_ref, dtb_ref, al_ref, ex_ref, y_ref, h_ref = refs
    q = SSM_CHUNK
    nch = xbc_ref.shape[1] // q
    nh = SSM_HEADS
    col0 = nh if reverse else 0
    log2e = 1.0 / math.log(2.0)

    @pl.when(pl.program_id(1) == 0)
    def _():
        h_ref[...] = jnp.zeros_like(h_ref)

    ii = lax.broadcasted_iota(jnp.int32, (q, q), 0)
    jj = lax.broadcasted_iota(jnp.int32, (q, q), 1)
    mask = (ii <= jj) if reverse else (ii >= jj)
    m_row = jnp.where((ii >= jj) if reverse else (ii <= jj), 1.0, 0.0).astype(bf16)
    hp = (nh // SSM_GROUPS) * SSM_HEAD_DIM
    first_head = lax.broadcasted_iota(jnp.int32, (q, LANES), 1) < SSM_HEAD_DIM

    dt_all = _softplus(dtt_ref[0, col0:col0 + nh, :] + dtb_ref[...])
    da_all = dt_all * (-jnp.exp(al_ref[...]))
    stack = lambda a: jnp.concatenate([a[:, c * q:(c + 1) * q] for c in range(nch)], axis=0)
    dt = stack(dt_all)
    cs2 = sum(_dot(part, m_row) for part in _split3(stack(da_all))) * log2e
    edge = cs2[:, 0:1] if reverse else cs2[:, q - 1:q]
    e = jnp.exp2(cs2)
    dd = dt * jnp.exp2(edge - cs2)
    rt = cs2 - jnp.log2(dt)
    e_hi = e.astype(bf16).astype(f32)
    dd_hi = dd.astype(bf16).astype(f32)
    pad = jnp.zeros((q - 5 * nh, q), f32)

    h = h_ref[...]
    for ci in range(nch):
        c = (nch - 1 - ci) if reverse else ci
        rows = slice(c * q, (c + 1) * q)
        hr = slice(c * nh, (c + 1) * nh)
        xs = xbc_ref[0, rows, 0:SSM_WIDTH]
        bm = xbc_ref[0, rows, SSM_WIDTH:SSM_WIDTH + BC_WIDTH].astype(bf16)
        cm = xbc_ref[0, rows, SSM_WIDTH + BC_WIDTH:XBC_WIDTH].astype(bf16)
        cols = jnp.concatenate([cs2[hr], e_hi[hr], (e - e_hi)[hr], dd_hi[hr], (dd - dd_hi)[hr], pad], axis=0).T
        wide = _dot(cols.astype(bf16), ex_ref[...])
        e_x, dd_x = wide[:, :SSM_WIDTH], wide[:, SSM_WIDTH:]
        xsb = xs.astype(bf16)
        xdd = (xs * dd_x).astype(bf16)
        hb = h.astype(bf16)
        ys, sts = [], []
        for g in range(SSM_GROUPS):
            bg = bm[:, g * SSM_STATE:(g + 1) * SSM_STATE]
            cg = cm[:, g * SSM_STATE:(g + 1) * SSM_STATE]
            gl = slice(g * hp, (g + 1) * hp)
            cb = _dot_nt(cg, bg)
            yoff = _dot(cg, hb[:, gl]) * e_x[:, gl]
            sts.append(_dot_tn(bg, xdd[:, gl]))
            yd = []
            for pr in range(nh // SSM_GROUPS // 2):
                ms = []
                for hd in (g * (nh // SSM_GROUPS) + 2 * pr, g * (nh // SSM_GROUPS) + 2 * pr + 1):
                    diff = cols[:, hd:hd + 1] - rt[c * nh + hd:c * nh + hd + 1, :]
                    ms.append((cb * jnp.exp2(jnp.where(mask, diff, -1e30))).astype(bf16))
                xp = xsb[:, g * hp + pr * LANES:g * hp + (pr + 1) * LANES]
                zero = jnp.zeros_like(xp)
                rhs = jnp.concatenate([jnp.where(first_head, xp, zero), jnp.where(first_head, zero, xp)], axis=0)
                yd.append(_dot(jnp.concatenate(ms, axis=1), rhs))
            ys.append(jnp.concatenate(yd, axis=1) + yoff)
        y = jnp.concatenate(ys, axis=1)
        edge_x = e_x[0:1, :] if reverse else e_x[q - 1:q, :]
        h = h * edge_x + jnp.concatenate(sts, axis=1)
        if final:
            tot = yf_ref[0, rows, :] + y + xs * dskip_ref[...]
            gated = tot * _silu(gate_ref[0, rows, :])
            y_ref[0, rows, :] = _rms(gated, nw_ref[...]).astype(y_ref.dtype)
        else:
            y_ref[0, rows, :] = y
    h_ref[...] = h


def _ssd(xbc, dtt, dtb, alog, expand, *, reverse, final_args=None):
    b, l, _ = xbc.shape
    tr = SSD_ROWS
    nt = l // tr
    final = final_args is not None
    d = 1 if reverse else 0

    def blk(i):
        return (nt - 1 - i) if reverse else i

    row3 = lambda w: pl.BlockSpec((1, tr, w), lambda bi, i: (bi, blk(i), 0))
    small = (dtb[d].reshape(SSM_HEADS, 1), alog[d].reshape(SSM_HEADS, 1), expand)
    args = [xbc, dtt, *small]
    in_specs = [row3(XBC_WIDTH), pl.BlockSpec((1, DT_WIDTH, tr), lambda bi, i: (bi, 0, blk(i))),
                *[_const_spec(a.shape) for a in small]]
    if final:
        yf, gate, dskip, nw = final_args
        args += [yf, gate, dskip, nw]
        in_specs += [row3(SSM_WIDTH), row3(SSM_WIDTH), _const_spec(dskip.shape), _const_spec(nw.shape)]
    return pl.pallas_call(
        functools.partial(_ssd_kernel, reverse=reverse, final=final),
        grid=(b, nt), in_specs=in_specs, out_specs=row3(SSM_WIDTH),
        out_shape=jax.ShapeDtypeStruct((b, l, SSM_WIDTH), bf16 if final else f32),
        scratch_shapes=[pltpu.VMEM((SSM_STATE, SSM_WIDTH), f32)],
        compiler_params=_cparams("parallel", "arbitrary"), name="ssd_bwd" if reverse else "ssd_fwd",
    )(*args)


def _ssd_expand_matrix():
    row = jnp.arange(SSM_CHUNK, dtype=jnp.int32)[:, None]
    lane = jnp.arange(2 * SSM_WIDTH, dtype=jnp.int32)[None, :]
    head = (lane % SSM_WIDTH) // SSM_HEAD_DIM
    quantity = lane // SSM_WIDTH
    base = SSM_HEADS + 2 * SSM_HEADS * quantity
    hit = jnp.logical_or(row == base + head, row == base + SSM_HEADS + head)
    return hit.astype(bf16)


def _kv_kernel(mem_ref, nw_ref, wkv_ref, k_ref, v_ref):
    mn = _rms(mem_ref[0], nw_ref[...]).astype(bf16)
    k_ref[0] = _dot(mn, wkv_ref[:, :D_MODEL]).astype(bf16)
    v_ref[0] = _dot(mn, wkv_ref[:, D_MODEL:]).astype(bf16)


def _kv_proj(mem, nw, wkv):
    b, m, d = mem.shape
    spec = pl.BlockSpec((1, m, d), lambda bi: (bi, 0, 0))
    return pl.pallas_call(
        _kv_kernel, grid=(b,), in_specs=[spec, _const_spec(nw.shape), _const_spec(wkv.shape)],
        out_specs=(spec, spec), out_shape=(jax.ShapeDtypeStruct((b, m, d), bf16),) * 2,
        compiler_params=_cparams("parallel"), name="kv_proj",
    )(mem, nw, wkv)


def _mix_attn_kernel(x_ref, x0_ref, z_ref, zc_ref, ys_ref, hb_ref, hn_ref, wout_ref, nx_ref, wq_ref,
                     k_ref, v_ref, wo_ref, o_ref):
    z = jnp.concatenate([z_ref[0, s] for s in range(HY_SLABS)], axis=1)
    zc = jnp.concatenate([zc_ref[0, s] for s in range(HY_SLABS)], axis=1)
    y_hy = _rms(x0_ref[0] * (zc + z * hb_ref[...]), hn_ref[...]).astype(bf16)
    x1 = x_ref[0] + _dot(y_hy, wout_ref[:HY_WIDTH, :]) + _dot(ys_ref[0], wout_ref[HY_WIDTH:, :])
    xn = _rms(x1, nx_ref[...]).astype(bf16)
    qa = (_dot(xn, wq_ref[...]) * (XA_HEAD_DIM ** -0.5)).astype(bf16)
    heads = []
    for h in range(XA_HEADS):
        hs = slice(h * XA_HEAD_DIM, (h + 1) * XA_HEAD_DIM)
        s = _dot_nt(qa[:, hs], k_ref[0, :, hs])
        e = jnp.exp(s - jnp.max(s, axis=-1, keepdims=True))
        den = jnp.sum(e, axis=-1, keepdims=True)
        heads.append(_dot(e.astype(bf16), v_ref[0, :, hs]) / den)
    o = jnp.concatenate(heads, axis=1).astype(bf16)
    o_ref[0] = x1 + _dot(o, wo_ref[...])


def _mix_attn(x, x0, z4, zc4, ys, hbias, hnorm, wout, nx, wq, k, v, wo):
    b, l, d = x.shape
    tm = ROW_TILE
    row = lambda w: pl.BlockSpec((1, tm, w), lambda bi, i: (bi, i, 0))
    slab = pl.BlockSpec((1, HY_SLABS, tm, LANES), lambda bi, i: (bi, 0, i, 0))
    memspec = pl.BlockSpec((1, N_MEM, d), lambda bi, i: (bi, 0, 0))
    consts = (hbias, hnorm, wout, nx, wq)
    return pl.pallas_call(
        _mix_attn_kernel, grid=(b, l // tm),
        in_specs=[row(d), row(HY_WIDTH), slab, slab, row(SSM_WIDTH), *[_const_spec(a.shape) for a in consts],
                  memspec, memspec, _const_spec(wo.shape)],
        out_specs=row(d), out_shape=jax.ShapeDtypeStruct((b, l, d), f32),
        compiler_params=_cparams("parallel", "parallel"), name="mix_attn",
    )(x, x0, z4, zc4, ys, *consts, k, v, wo)


def _mlp_kernel(x_ref, nw_ref, wup_ref, wdn_ref, fw_ref, o_ref, *, final):
    x = x_ref[0]
    xn = _rms(x, nw_ref[...]).astype(bf16)
    acc = x
    for c in range(D_FF // D_MODEL):
        cs = slice(c * D_MODEL, (c + 1) * D_MODEL)
        hdn = jnp.maximum(_dot(xn, wup_ref[:, cs]), 0.0)
        acc = acc + _dot((hdn * hdn).astype(bf16), wdn_ref[cs, :])
    o_ref[0] = _rms(acc, fw_ref[...]) if final else acc


def _mlp(x, nw, wup, wdn, fw, final):
    b, l, d = x.shape
    tm = ROW_TILE
    row = pl.BlockSpec((1, tm, d), lambda bi, i: (bi, i, 0))
    wspec = lambda a: pl.BlockSpec(a.shape, lambda bi, i: (0, 0), pipeline_mode=pl.Buffered(1))
    return pl.pallas_call(
        functools.partial(_mlp_kernel, final=final), grid=(b, l // tm),
        in_specs=[row, _const_spec(nw.shape), wspec(wup), wspec(wdn), _const_spec(fw.shape)],
        out_specs=row, out_shape=jax.ShapeDtypeStruct((b, l, d), f32),
        compiler_params=_cparams("parallel", "parallel"), name="mlp",
    )(x, nw, wup, wdn, fw)


def _trunk(x, mem, p, tables):
    tc, ts, ic, isn = tables
    seq_len = x.shape[1]
    for i in range(DEPTH):
        lp = p["layers"][i]
        x0, z4, gate, xbc, dtt = _inproj(x, lp["norm_mix"], lp["w_main"], lp["w_dtt"],
                                         lp["hy_conv_w"], lp["hy_conv_b"], lp["ssm_conv_w"], lp["ssm_conv_b"])
        kre, kim = _filter_spectra(seq_len, lp["hy_fw1"], lp["hy_fb1"], lp["hy_fw2"], lp["hy_fb2"],
                                   lp["hy_fw3"], lp["hy_fb3"], lp["hy_sin_freq"], tc, ts)
        zc4 = _longconv(z4, kre, kim, tc, ts, ic, isn)
        yf = _ssd(xbc, dtt, lp["ssm_dt_bias"], lp["ssm_A_log"], p["ssd_expand"], reverse=False)
        ys = _ssd(xbc, dtt, lp["ssm_dt_bias"], lp["ssm_A_log"], p["ssd_expand"], reverse=True,
                  final_args=(yf, gate, lp["ssm_D_wide"], lp["ssm_norm"]))
        k, v = _kv_proj(mem, lp["norm_mem"], lp["w_kv"])
        x = _mix_attn(x, x0, z4, zc4, ys, lp["hy_bias"], lp["hy_norm"], lp["w_out"], lp["norm_xattn"],
                      lp["w_q"], k, v, lp["w_o"])
        x = _mlp(x, lp["norm_mlp"], lp["w_up"], lp["w_down"], p["norm_final"], final=(i == DEPTH - 1))
    return x


def kernel(x_prompt, x_sample, mem_prompt, mem_sample, norm_mix, w_in, hy_conv_w, hy_conv_b, hy_fw1, hy_fb1, hy_fw2, hy_fb2, hy_fw3, hy_fb3, hy_sin_freq, hy_bias, hy_norm, ssm_conv_w, ssm_conv_b, ssm_dt_bias, ssm_A_log, ssm_D, ssm_norm, w_out, norm_xattn, norm_mem, w_q, w_kv, w_o, norm_mlp, w_up, w_down, norm_final):
    row = lambda a: a.reshape(1, -1)
    layers = []
    for i in range(DEPTH):
        wi = w_in[i].astype(bf16)
        layers.append(dict(
            norm_mix=row(norm_mix[i]), w_main=wi[:, :MAIN_IN], w_dtt=wi[:, MAIN_IN:].T,
            hy_conv_w=hy_conv_w[i], hy_conv_b=row(hy_conv_b[i]),
            ssm_conv_w=ssm_conv_w[i], ssm_conv_b=row(ssm_conv_b[i]),
            hy_fw1=hy_fw1[i], hy_fb1=hy_fb1[i], hy_fw2=hy_fw2[i], hy_fb2=hy_fb2[i], hy_fw3=hy_fw3[i],
            hy_fb3=hy_fb3[i], hy_sin_freq=hy_sin_freq[i], hy_bias=row(hy_bias[i]), hy_norm=row(hy_norm[i]),
            ssm_dt_bias=ssm_dt_bias[i], ssm_A_log=ssm_A_log[i],
            ssm_D_wide=row(jnp.repeat(ssm_D[i], SSM_HEAD_DIM)), ssm_norm=row(ssm_norm[i]),
            w_out=w_out[i].astype(bf16), norm_xattn=row(norm_xattn[i]), norm_mem=row(norm_mem[i]),
            w_q=w_q[i].astype(bf16), w_kv=w_kv[i].astype(bf16), w_o=w_o[i].astype(bf16),
            norm_mlp=row(norm_mlp[i]), w_up=w_up[i].astype(bf16), w_down=w_down[i].astype(bf16),
        ))
    p = dict(layers=layers, norm_final=row(norm_final), ssd_expand=_ssd_expand_matrix())
    tables = _dft_tables()
    return (_trunk(x_prompt, mem_prompt, p, tables), _trunk(x_sample, mem_sample, p, tables))
```

```python
import functools
import math

import jax
import jax.numpy as jnp
from jax import lax
from jax.experimental import pallas as pl
from jax.experimental.pallas import tpu as pltpu

f32 = jnp.float32
bf16 = jnp.bfloat16

D_MODEL = 1024
DEPTH = 2
N_MEM = 256
HY_WIDTH = 512
HY_CONV = 3
HY_BANDS = 16
HY_ORDER = 64
HY_DECAY_TARGET = 1e-2
HY_FAST_PCT = 0.3
HY_SLOW_PCT = 1.5
SSM_WIDTH = 512
SSM_HEAD_DIM = 64
SSM_HEADS = 8
SSM_GROUPS = 2
SSM_STATE = 128
SSM_CONV = 5
SSM_CHUNK = 128
BC_WIDTH = SSM_GROUPS * SSM_STATE
XBC_WIDTH = SSM_WIDTH + 2 * BC_WIDTH
HY_IN = 3 * HY_WIDTH
MAIN_IN = HY_IN + SSM_WIDTH + XBC_WIDTH
DT_WIDTH = 2 * SSM_HEADS
XA_HEADS = 4
XA_HEAD_DIM = D_MODEL // XA_HEADS
D_FF = 4 * D_MODEL
EPS = 1e-5

LANES = 128
SUBLANES = 8
HALO = SUBLANES
SUB_LEN = 512
SUB_FFT = 2 * SUB_LEN
HY_SLABS = HY_WIDTH // LANES
ROW_TILE = 512
INPROJ_TILE = 1024
SSD_ROWS = 2048
PW_ROWS = 32
VMEM_LIMIT = 56 * 1024 * 1024


def _cparams(*sem):
    return pltpu.CompilerParams(dimension_semantics=sem, vmem_limit_bytes=VMEM_LIMIT)


def _rms(x, w):
    return x * lax.rsqrt(jnp.mean(x * x, axis=-1, keepdims=True) + EPS) * w


def _dot(a, b):
    return jnp.dot(a, b, preferred_element_type=f32)


def _dot_hi(a, b):
    return jnp.dot(a, b, preferred_element_type=f32, precision=lax.Precision.HIGHEST)


def _dot_3pass(a, b):
    a_hi = a.astype(bf16)
    a_lo = (a - a_hi.astype(f32)).astype(bf16)
    b_hi = b.astype(bf16)
    b_lo = (b - b_hi.astype(f32)).astype(bf16)
    return _dot(a_hi, b_hi) + _dot(a_hi, b_lo) + _dot(a_lo, b_hi)


def _dot_nt(a, b):
    return lax.dot_general(a, b, (((1,), (1,)), ((), ())), preferred_element_type=f32)


def _dot_tn(a, b):
    return lax.dot_general(a, b, (((0,), (0,)), ((), ())), preferred_element_type=f32)


def _split3(x):
    hi = x.astype(bf16)
    r1 = x - hi.astype(f32)
    mid = r1.astype(bf16)
    lo = (r1 - mid.astype(f32)).astype(bf16)
    return hi, mid, lo


def _silu(x):
    return x / (1.0 + jnp.exp(-x))


def _softplus(x):
    return jnp.maximum(x, 0.0) + jnp.log(1.0 + jnp.exp(-jnp.abs(x)))


def _const_spec(shape):
    nd = len(shape)
    return pl.BlockSpec(shape, lambda *_: (0,) * nd)


def _inproj_kernel(xp_ref, x_ref, xn_ref, nw_ref, w_ref, wdtt_ref, hcw_ref, hcb_ref,
                   scw_ref, scb_ref, x0_ref, z_ref, gate_ref, xbc_ref, dtt_ref, pbuf):
    tm = x_ref.shape[1]
    i = pl.program_id(1)
    nw = nw_ref[...]
    xp = jnp.where(i > 0, _rms(xp_ref[0], nw), 0.0)
    xn = jnp.where(i < pl.num_programs(1) - 1, _rms(xn_ref[0], nw), 0.0)
    xm = _rms(x_ref[0], nw)
    xe = jnp.concatenate([xp, xm, xn], axis=0).astype(bf16)
    xmb = xe[HALO:HALO + tm]

    def conv_slabs(buf, col0, cw_ref, cb_ref, taps, wcol0):
        p = _dot(xe, w_ref[:, col0:col0 + HY_WIDTH])
        for s in range(HY_SLABS):
            pbuf[buf, s] = p[:, s * LANES:(s + 1) * LANES]
        pad = taps // 2
        outs = []
        for s in range(HY_SLABS):
            cs = slice(wcol0 + s * LANES, wcol0 + (s + 1) * LANES)
            acc = cb_ref[:, cs]
            for k in range(taps):
                acc = acc + pbuf[buf, s, pl.ds(HALO + k - pad, tm), :] * cw_ref[k:k + 1, cs]
            outs.append(acc)
        return outs

    for s, blk in enumerate(conv_slabs(0, 0, hcw_ref, hcb_ref, HY_CONV, 0)):
        x0_ref[0, :, s * LANES:(s + 1) * LANES] = blk
    for s, blk in enumerate(conv_slabs(1, HY_WIDTH, hcw_ref, hcb_ref, HY_CONV, HY_WIDTH)):
        z_ref[0, s] = blk
    for s, blk in enumerate(conv_slabs(0, 2 * HY_WIDTH, hcw_ref, hcb_ref, HY_CONV, 2 * HY_WIDTH)):
        z_ref[0, s] = z_ref[0, s] * blk
    gate_ref[0] = _dot(xmb, w_ref[:, HY_IN:HY_IN + SSM_WIDTH])
    o2 = HY_IN + SSM_WIDTH
    for c in range(XBC_WIDTH // HY_WIDTH):
        for s, blk in enumerate(conv_slabs(1 - c % 2, o2 + c * HY_WIDTH, scw_ref, scb_ref, SSM_CONV, c * HY_WIDTH)):
            xbc_ref[0, :, c * HY_WIDTH + s * LANES:c * HY_WIDTH + (s + 1) * LANES] = _silu(blk)
    dtt_ref[0] = _dot_nt(wdtt_ref[...], xmb)


def _inproj(x, nw, w_main, w_dtt, hcw, hcb, scw, scb):
    b, l, d = x.shape
    tm = INPROJ_TILE
    nt = l // tm
    hb = tm // HALO
    grid = (b, nt)
    out_shape = (
        jax.ShapeDtypeStruct((b, l, HY_WIDTH), f32),
        jax.ShapeDtypeStruct((b, HY_SLABS, l, LANES), f32),
        jax.ShapeDtypeStruct((b, l, SSM_WIDTH), f32),
        jax.ShapeDtypeStruct((b, l, XBC_WIDTH), f32),
        jax.ShapeDtypeStruct((b, DT_WIDTH, l), f32),
    )
    in_specs = [
        pl.BlockSpec((1, HALO, d), lambda bi, i: (bi, jnp.maximum(i * hb - 1, 0), 0)),
        pl.BlockSpec((1, tm, d), lambda bi, i: (bi, i, 0)),
        pl.BlockSpec((1, HALO, d), lambda bi, i: (bi, jnp.minimum((i + 1) * hb, l // HALO - 1), 0)),
        _const_spec(nw.shape), _const_spec(w_main.shape),
        _const_spec(w_dtt.shape), _const_spec(hcw.shape), _const_spec(hcb.shape),
        _const_spec(scw.shape), _const_spec(scb.shape),
    ]
    out_specs = (
        pl.BlockSpec((1, tm, HY_WIDTH), lambda bi, i: (bi, i, 0)),
        pl.BlockSpec((1, HY_SLABS, tm, LANES), lambda bi, i: (bi, 0, i, 0)),
        pl.BlockSpec((1, tm, SSM_WIDTH), lambda bi, i: (bi, i, 0)),
        pl.BlockSpec((1, tm, XBC_WIDTH), lambda bi, i: (bi, i, 0)),
        pl.BlockSpec((1, DT_WIDTH, tm), lambda bi, i: (bi, 0, i)),
    )
    return pl.pallas_call(
        _inproj_kernel, grid=grid, in_specs=in_specs, out_specs=out_specs, out_shape=out_shape,
        scratch_shapes=[pltpu.VMEM((2, HY_SLABS, tm + 2 * HALO, LANES), f32)],
        compiler_params=_cparams("parallel", "parallel"), name="inproj",
    )(x, x, x, nw, w_main, w_dtt, hcw, hcb, scw, scb)


def _spectra_kernel(fw1t_ref, fw1c_ref, fw1s_ref, fb1_ref, fw2_ref, fb2_ref, fw3_ref, fb3_ref,
                    freq_ref, tc_ref, ts_ref, kre_ref, kim_ref, *, seq_len, phases):
    d = pl.program_id(0)
    lane = lax.broadcasted_iota(jnp.int32, (1, 2 * SUB_LEN), 1)
    m = lane % SUB_LEN
    pos = jnp.abs(jnp.where(lane >= SUB_LEN, d - phases * m, d + phases * m)).astype(f32)
    t = pos * (1.0 / (seq_len - 1))
    w = pos * (2.0 * math.pi / seq_len)
    band = lax.broadcasted_iota(jnp.int32, (HY_BANDS, 1), 0).astype(f32)
    fband = 1e-4 + band * ((HY_BANDS - 1 - 1e-4) / (HY_BANDS - 1))
    ang = fband * w
    fr = freq_ref[...]
    pre = fw1t_ref[...] * t + _dot_hi(fw1c_ref[...], jnp.cos(ang)) - _dot_hi(fw1s_ref[...], jnp.sin(ang))
    h = jnp.sin(fr * (pre + fb1_ref[...]))
    h = jnp.sin(fr * (_dot_hi(fw2_ref[...], h) + fb2_ref[...]))
    o = _dot_3pass(fw3_ref[...], h) + fb3_ref[...]
    max_decay = math.log(HY_DECAY_TARGET) / HY_FAST_PCT
    min_decay = math.log(HY_DECAY_TARGET) / HY_SLOW_PCT
    ch = lax.broadcasted_iota(jnp.int32, (HY_WIDTH, 1), 0).astype(f32)
    deltas = jnp.abs(min_decay + ch * ((max_decay - min_decay) / (HY_WIDTH - 1)))
    decay = jnp.exp(-deltas * t)
    fwd = o[:HY_WIDTH] * decay
    bwd = o[HY_WIDTH:] * decay
    lag0 = m[:, :SUB_LEN] == 0
    gp = fwd[:, :SUB_LEN]
    gm = jnp.where(lag0, 0.0, bwd[:, SUB_LEN:])
    kre_ref[0, 0] = _dot_nt(tc_ref[...], (gp + gm).astype(bf16))
    kim_ref[0, 0] = _dot_nt(ts_ref[...], (gp - gm).astype(bf16))
    gp = jnp.where(jnp.logical_and(lag0, d > 0), bwd[:, SUB_LEN:], fwd[:, SUB_LEN:])
    gm = jnp.where(lag0, 0.0, bwd[:, :SUB_LEN])
    kre_ref[1, 0] = _dot_nt(tc_ref[...], (gp + gm).astype(bf16))
    kim_ref[1, 0] = _dot_nt(ts_ref[...], (gp - gm).astype(bf16))


def _filter_spectra(seq_len, fw1, fb1, fw2, fb2, fw3, fb3, freq, tc, ts):
    phases = seq_len // SUB_LEN
    col = lambda a: a.reshape(-1, 1)
    args = (col(fw1[0]), fw1[1:1 + HY_BANDS].T, fw1[1 + HY_BANDS:].T, col(fb1), fw2.T, col(fb2), fw3.T, col(fb3),
            col(freq), tc, ts)
    out_shape = (jax.ShapeDtypeStruct((2, phases, SUB_LEN, HY_WIDTH), f32),) * 2
    spec = pl.BlockSpec((2, 1, SUB_LEN, HY_WIDTH), lambda k: (0, k, 0, 0))
    return pl.pallas_call(
        functools.partial(_spectra_kernel, seq_len=seq_len, phases=phases),
        grid=(phases,), in_specs=[_const_spec(a.shape) for a in args], out_specs=(spec, spec),
        out_shape=out_shape, compiler_params=_cparams("parallel"), name="filter_spectra",
    )(*args)


def _dft_tables():
    f = jnp.arange(SUB_LEN, dtype=jnp.int32)[:, None]
    n = jnp.arange(SUB_LEN, dtype=jnp.int32)[None, :]
    ang = (((2 * f + 1) * n) % (2 * SUB_FFT)).astype(f32) * (math.pi / SUB_FFT)
    c, s = jnp.cos(ang), jnp.sin(ang)
    tc, ts = c.astype(bf16), (-s).astype(bf16)
    scale = 2.0 / SUB_FFT
    ic = (scale * c).T.astype(bf16)
    isn = (-scale * s).T.astype(bf16)
    return tc, ts, ic, isn


def _longconv_kernel(z_ref, kre_ref, kim_ref, tc_ref, ts_ref, ic_ref, is_ref, o_ref,
                     zre, zim, wre, wim, *, phases):
    slabs = z_ref.shape[1]
    for q in range(phases):
        zq = jnp.concatenate(
            [z_ref[0, s, pl.ds(q, SUB_LEN, stride=phases), :] for s in range(slabs)], axis=1).astype(bf16)
        zre[q] = _dot(tc_ref[...], zq)
        zim[q] = _dot(ts_ref[...], zq)

    def pointwise(p, slot):
        for r in range(SUB_LEN // PW_ROWS):
            rs = slice(r * PW_ROWS, (r + 1) * PW_ROWS)
            are = jnp.zeros((PW_ROWS, zre.shape[2]), f32)
            aim = jnp.zeros((PW_ROWS, zre.shape[2]), f32)
            for q in range(phases):
                kd = jnp.where(p >= q, p - q, phases + q - p)
                kr, ki = kre_ref[kd, rs, :], kim_ref[kd, rs, :]
                xr, xi = zre[q, rs, :], zim[q, rs, :]
                are = are + (kr * xr - ki * xi)
                aim = aim + (kr * xi + ki * xr)
            wre[slot, rs, :] = are.astype(bf16)
            wim[slot, rs, :] = aim.astype(bf16)

    def inverse(p, slot):
        yp = _dot(ic_ref[...], wre[slot]) + _dot(is_ref[...], wim[slot])
        for s in range(slabs):
            o_ref[0, s, pl.ds(p, SUB_LEN, stride=phases), :] = yp[:, s * LANES:(s + 1) * LANES]

    pointwise(0, 0)

    def body(p, carry):
        pointwise(p, p % 2)
        inverse(p - 1, (p - 1) % 2)
        return carry

    lax.fori_loop(1, phases, body, 0)
    inverse(phases - 1, (phases - 1) % 2)


def _longconv(z4, kre, kim, tcf, tsf, ic, isn):
    b, slabs, l, _ = z4.shape
    phases = l // SUB_LEN
    sb = 2
    cb = sb * LANES
    zspec = pl.BlockSpec((1, sb, l, LANES), lambda j, bi: (bi, j, 0, 0))
    kspec = pl.BlockSpec((2 * phases, SUB_LEN, cb), lambda j, bi: (0, 0, j), pipeline_mode=pl.Buffered(1))
    tspec = _const_spec(tcf.shape)
    return pl.pallas_call(
        functools.partial(_longconv_kernel, phases=phases),
        grid=(slabs // sb, b),
        in_specs=[zspec, kspec, kspec, tspec, tspec, tspec, tspec], out_specs=zspec,
        out_shape=jax.ShapeDtypeStruct(z4.shape, f32),
        scratch_shapes=[pltpu.VMEM((phases, SUB_LEN, cb), f32), pltpu.VMEM((phases, SUB_LEN, cb), f32),
                        pltpu.VMEM((2, SUB_LEN, cb), bf16), pltpu.VMEM((2, SUB_LEN, cb), bf16)],
        compiler_params=_cparams("parallel", "parallel"), name="longconv",
    )(z4, kre.reshape(2 * phases, SUB_LEN, HY_WIDTH), kim.reshape(2 * phases, SUB_LEN, HY_WIDTH),
      tcf, tsf, ic, isn)


def _ssd_kernel(*refs, reverse, final):
    if final:
        xbc_ref, dtt_ref, dtb_ref, al_ref, ex_ref, yf_ref, gate_ref, dskip_ref, nw_ref, y_ref, h_ref = refs
    else:
        xbc_ref, dtt_ref, dtb_ref, al_ref, ex_ref, y_ref, h_ref = refs
    q = SSM_CHUNK
    nch = xbc_ref.shape[1] // q
    nh = SSM_HEADS
    col0 = nh if reverse else 0
    log2e = 1.0 / math.log(2.0)

    @pl.when(pl.program_id(1) == 0)
    def _():
        h_ref[...] = jnp.zeros_like(h_ref)

    ii = lax.broadcasted_iota(jnp.int32, (q, q), 0)
    jj = lax.broadcasted_iota(jnp.int32, (q, q), 1)
    mask = (ii <= jj) if reverse else (ii >= jj)
    m_row = jnp.where((ii >= jj) if reverse else (ii <= jj), 1.0, 0.0).astype(bf16)
    hp = (nh // SSM_GROUPS) * SSM_HEAD_DIM
    first_head = lax.broadcasted_iota(jnp.int32, (q, LANES), 1) < SSM_HEAD_DIM

    dt_all = _softplus(dtt_ref[0, col0:col0 + nh, :] + dtb_ref[...])
    da_all = dt_all * (-jnp.exp(al_ref[...]))
    stack = lambda a: jnp.concatenate([a[:, c * q:(c + 1) * q] for c in range(nch)], axis=0)
    dt = stack(dt_all)
    cs2 = sum(_dot(part, m_row) for part in _split3(stack(da_all))) * log2e
    edge = cs2[:, 0:1] if reverse else cs2[:, q - 1:q]
    e = jnp.exp2(cs2)
    dd = dt * jnp.exp2(edge - cs2)
    rt = cs2 - jnp.log2(dt)
    e_hi = e.astype(bf16).astype(f32)
    dd_hi = dd.astype(bf16).astype(f32)
    pad = jnp.zeros((q - 5 * nh, q), f32)

    h = h_ref[...]
    for ci in range(nch):
        c = (nch - 1 - ci) if reverse else ci
        rows = slice(c * q, (c + 1) * q)
        hr = slice(c * nh, (c + 1) * nh)
        xs = xbc_ref[0, rows, 0:SSM_WIDTH]
        bm = xbc_ref[0, rows, SSM_WIDTH:SSM_WIDTH + BC_WIDTH].astype(bf16)
        cm = xbc_ref[0, rows, SSM_WIDTH + BC_WIDTH:XBC_WIDTH].astype(bf16)
        cols = jnp.concatenate([cs2[hr], e_hi[hr], (e - e_hi)[hr], dd_hi[hr], (dd - dd_hi)[hr], pad], axis=0).T
        wide = _dot(cols.astype(bf16), ex_ref[...])
        e_x, dd_x = wide[:, :SSM_WIDTH], wide[:, SSM_WIDTH:]
        xsb = xs.astype(bf16)
        xdd = (xs * dd_x).astype(bf16)
        hb = h.astype(bf16)
        ys, sts = [], []
        cstack = jnp.concatenate([cm[:, :SSM_STATE], cm[:, SSM_STATE:]], axis=0)
        bstack = jnp.concatenate([bm[:, :SSM_STATE], bm[:, SSM_STATE:]], axis=0)
        cb_all = _dot_nt(cstack, bstack)
        zh = jnp.zeros((SSM_STATE, hp), bf16)
        h_blocks = jnp.concatenate([jnp.concatenate([hb[:, :hp], zh], axis=1),
                                    jnp.concatenate([zh, hb[:, hp:]], axis=1)], axis=0)
        yoff_all = _dot(cm, h_blocks) * e_x
        for g in range(SSM_GROUPS):
            bg = bm[:, g * SSM_STATE:(g + 1) * SSM_STATE]
            gl = slice(g * hp, (g + 1) * hp)
            cb = cb_all[g * q:(g + 1) * q, g * q:(g + 1) * q]
            yoff = yoff_all[:, gl]
            sts.append(_dot_tn(bg, xdd[:, gl]))
            yd = []
            for pr in range(nh // SSM_GROUPS // 2):
                ms = []
                for hd in (g * (nh // SSM_GROUPS) + 2 * pr, g * (nh // SSM_GROUPS) + 2 * pr + 1):
                    diff = cols[:, hd:hd + 1] - rt[c * nh + hd:c * nh + hd + 1, :]
                    ms.append((cb * jnp.exp2(jnp.where(mask, diff, -1e30))).astype(bf16))
                xp = xsb[:, g * hp + pr * LANES:g * hp + (pr + 1) * LANES]
                zero = jnp.zeros_like(xp)
                rhs = jnp.concatenate([jnp.where(first_head, xp, zero), jnp.where(first_head, zero, xp)], axis=0)
                yd.append(_dot(jnp.concatenate(ms, axis=1), rhs))
            ys.append(jnp.concatenate(yd, axis=1) + yoff)
        y = jnp.concatenate(ys, axis=1)
        edge_x = e_x[0:1, :] if reverse else e_x[q - 1:q, :]
        h = h * edge_x + jnp.concatenate(sts, axis=1)
        if final:
            tot = yf_ref[0, rows, :] + y + xs * dskip_ref[...]
            gated = tot * _silu(gate_ref[0, rows, :])
            y_ref[0, rows, :] = _rms(gated, nw_ref[...]).astype(y_ref.dtype)
        else:
            y_ref[0, rows, :] = y
    h_ref[...] = h


def _ssd(xbc, dtt, dtb, alog, expand, *, reverse, final_args=None):
    b, l, _ = xbc.shape
    tr = SSD_ROWS
    nt = l // tr
    final = final_args is not None
    d = 1 if reverse else 0

    def blk(i):
        return (nt - 1 - i) if reverse else i

    row3 = lambda w: pl.BlockSpec((1, tr, w), lambda bi, i: (bi, blk(i), 0))
    small = (dtb[d].reshape(SSM_HEADS, 1), alog[d].reshape(SSM_HEADS, 1), expand)
    args = [xbc, dtt, *small]
    in_specs = [row3(XBC_WIDTH), pl.BlockSpec((1, DT_WIDTH, tr), lambda bi, i: (bi, 0, blk(i))),
                *[_const_spec(a.shape) for a in small]]
    if final:
        yf, gate, dskip, nw = final_args
        args += [yf, gate, dskip, nw]
        in_specs += [row3(SSM_WIDTH), row3(SSM_WIDTH), _const_spec(dskip.shape), _const_spec(nw.shape)]
    return pl.pallas_call(
        functools.partial(_ssd_kernel, reverse=reverse, final=final),
        grid=(b, nt), in_specs=in_specs, out_specs=row3(SSM_WIDTH),
        out_shape=jax.ShapeDtypeStruct((b, l, SSM_WIDTH), bf16 if final else f32),
        scratch_shapes=[pltpu.VMEM((SSM_STATE, SSM_WIDTH), f32)],
        compiler_params=_cparams("parallel", "arbitrary"), name="ssd_bwd" if reverse else "ssd_fwd",
    )(*args)


def _ssd_expand_matrix():
    row = jnp.arange(SSM_CHUNK, dtype=jnp.int32)[:, None]
    lane = jnp.arange(2 * SSM_WIDTH, dtype=jnp.int32)[None, :]
    head = (lane % SSM_WIDTH) // SSM_HEAD_DIM
    quantity = lane // SSM_WIDTH
    base = SSM_HEADS + 2 * SSM_HEADS * quantity
    hit = jnp.logical_or(row == base + head, row == base + SSM_HEADS + head)
    return hit.astype(bf16)


def _kv_kernel(mem_ref, nw_ref, wkv_ref, k_ref, v_ref):
    mn = _rms(mem_ref[0], nw_ref[...]).astype(bf16)
    k_ref[0] = _dot(mn, wkv_ref[:, :D_MODEL]).astype(bf16)
    v_ref[0] = _dot(mn, wkv_ref[:, D_MODEL:]).astype(bf16)


def _kv_proj(mem, nw, wkv):
    b, m, d = mem.shape
    spec = pl.BlockSpec((1, m, d), lambda bi: (bi, 0, 0))
    return pl.pallas_call(
        _kv_kernel, grid=(b,), in_specs=[spec, _const_spec(nw.shape), _const_spec(wkv.shape)],
        out_specs=(spec, spec), out_shape=(jax.ShapeDtypeStruct((b, m, d), bf16),) * 2,
        compiler_params=_cparams("parallel"), name="kv_proj",
    )(mem, nw, wkv)


def _mix_attn_kernel(x_ref, x0_ref, z_ref, zc_ref, ys_ref, hb_ref, hn_ref, wout_ref, nx_ref, wq_ref,
                     k_ref, v_ref, wo_ref, o_ref):
    z = jnp.concatenate([z_ref[0, s] for s in range(HY_SLABS)], axis=1)
    zc = jnp.concatenate([zc_ref[0, s] for s in range(HY_SLABS)], axis=1)
    y_hy = _rms(x0_ref[0] * (zc + z * hb_ref[...]), hn_ref[...]).astype(bf16)
    x1 = x_ref[0] + _dot(jnp.concatenate([y_hy, ys_ref[0]], axis=1), wout_ref[...])
    xn = _rms(x1, nx_ref[...]).astype(bf16)
    qa = (_dot(xn, wq_ref[...]) * (XA_HEAD_DIM ** -0.5)).astype(bf16)
    heads = []
    for h in range(XA_HEADS):
        hs = slice(h * XA_HEAD_DIM, (h + 1) * XA_HEAD_DIM)
        s = _dot_nt(qa[:, hs], k_ref[0, :, hs])
        e = jnp.exp(s - jnp.max(s, axis=-1, keepdims=True))
        den = jnp.sum(e, axis=-1, keepdims=True)
        heads.append(_dot(e.astype(bf16), v_ref[0, :, hs]) / den)
    o = jnp.concatenate(heads, axis=1).astype(bf16)
    o_ref[0] = x1 + _dot(o, wo_ref[...])


def _mix_attn(x, x0, z4, zc4, ys, hbias, hnorm, wout, nx, wq, k, v, wo):
    b, l, d = x.shape
    tm = ROW_TILE
    row = lambda w: pl.BlockSpec((1, tm, w), lambda bi, i: (bi, i, 0))
    slab = pl.BlockSpec((1, HY_SLABS, tm, LANES), lambda bi, i: (bi, 0, i, 0))
    memspec = pl.BlockSpec((1, N_MEM, d), lambda bi, i: (bi, 0, 0))
    consts = (hbias, hnorm, wout, nx, wq)
    return pl.pallas_call(
        _mix_attn_kernel, grid=(b, l // tm),
        in_specs=[row(d), row(HY_WIDTH), slab, slab, row(SSM_WIDTH), *[_const_spec(a.shape) for a in consts],
                  memspec, memspec, _const_spec(wo.shape)],
        out_specs=row(d), out_shape=jax.ShapeDtypeStruct((b, l, d), f32),
        compiler_params=_cparams("parallel", "parallel"), name="mix_attn",
    )(x, x0, z4, zc4, ys, *consts, k, v, wo)


def _mlp_kernel(x_ref, nw_ref, wup_ref, wdn_ref, fw_ref, o_ref, *, final):
    x = x_ref[0]
    xn = _rms(x, nw_ref[...]).astype(bf16)
    acc = x
    for c in range(D_FF // D_MODEL):
        cs = slice(c * D_MODEL, (c + 1) * D_MODEL)
        hdn = jnp.maximum(_dot(xn, wup_ref[:, cs]), 0.0)
        acc = acc + _dot((hdn * hdn).astype(bf16), wdn_ref[cs, :])
    o_ref[0] = _rms(acc, fw_ref[...]) if final else acc


def _mlp(x, nw, wup, wdn, fw, final):
    b, l, d = x.shape
    tm = ROW_TILE
    row = pl.BlockSpec((1, tm, d), lambda bi, i: (bi, i, 0))
    wspec = lambda a: pl.BlockSpec(a.shape, lambda bi, i: (0, 0), pipeline_mode=pl.Buffered(1))
    return pl.pallas_call(
        functools.partial(_mlp_kernel, final=final), grid=(b, l // tm),
        in_specs=[row, _const_spec(nw.shape), wspec(wup), wspec(wdn), _const_spec(fw.shape)],
        out_specs=row, out_shape=jax.ShapeDtypeStruct((b, l, d), f32),
        compiler_params=_cparams("parallel", "parallel"), name="mlp",
    )(x, nw, wup, wdn, fw)


def _trunk(x, mem, p, tables):
    tc, ts, ic, isn = tables
    seq_len = x.shape[1]
    for i in range(DEPTH):
        lp = p["layers"][i]
        x0, z4, gate, xbc, dtt = _inproj(x, lp["norm_mix"], lp["w_main"], lp["w_dtt"],
                                         lp["hy_conv_w"], lp["hy_conv_b"], lp["ssm_conv_w"], lp["ssm_conv_b"])
        kre, kim = _filter_spectra(seq_len, lp["hy_fw1"], lp["hy_fb1"], lp["hy_fw2"], lp["hy_fb2"],
                                   lp["hy_fw3"], lp["hy_fb3"], lp["hy_sin_freq"], tc, ts)
        zc4 = _longconv(z4, kre, kim, tc, ts, ic, isn)
        yf = _ssd(xbc, dtt, lp["ssm_dt_bias"], lp["ssm_A_log"], p["ssd_expand"], reverse=False)
        ys = _ssd(xbc, dtt, lp["ssm_dt_bias"], lp["ssm_A_log"], p["ssd_expand"], reverse=True,
                  final_args=(yf, gate, lp["ssm_D_wide"], lp["ssm_norm"]))
        k, v = _kv_proj(mem, lp["norm_mem"], lp["w_kv"])
        x = _mix_attn(x, x0, z4, zc4, ys, lp["hy_bias"], lp["hy_norm"], lp["w_out"], lp["norm_xattn"],
                      lp["w_q"], k, v, lp["w_o"])
        x = _mlp(x, lp["norm_mlp"], lp["w_up"], lp["w_down"], p["norm_final"], final=(i == DEPTH - 1))
    return x


def kernel(x_prompt, x_sample, mem_prompt, mem_sample, norm_mix, w_in, hy_conv_w, hy_conv_b, hy_fw1, hy_fb1, hy_fw2, hy_fb2, hy_fw3, hy_fb3, hy_sin_freq, hy_bias, hy_norm, ssm_conv_w, ssm_conv_b, ssm_dt_bias, ssm_A_log, ssm_D, ssm_norm, w_out, norm_xattn, norm_mem, w_q, w_kv, w_o, norm_mlp, w_up, w_down, norm_final):
    row = lambda a: a.reshape(1, -1)
    layers = []
    for i in range(DEPTH):
        wi = w_in[i].astype(bf16)
        layers.append(dict(
            norm_mix=row(norm_mix[i]), w_main=wi[:, :MAIN_IN], w_dtt=wi[:, MAIN_IN:].T,
            hy_conv_w=hy_conv_w[i], hy_conv_b=row(hy_conv_b[i]),
            ssm_conv_w=ssm_conv_w[i], ssm_conv_b=row(ssm_conv_b[i]),
            hy_fw1=hy_fw1[i], hy_fb1=hy_fb1[i], hy_fw2=hy_fw2[i], hy_fb2=hy_fb2[i], hy_fw3=hy_fw3[i],
            hy_fb3=hy_fb3[i], hy_sin_freq=hy_sin_freq[i], hy_bias=row(hy_bias[i]), hy_norm=row(hy_norm[i]),
            ssm_dt_bias=ssm_dt_bias[i], ssm_A_log=ssm_A_log[i],
            ssm_D_wide=row(jnp.repeat(ssm_D[i], SSM_HEAD_DIM)), ssm_norm=row(ssm_norm[i]),
            w_out=w_out[i].astype(bf16), norm_xattn=row(norm_xattn[i]), norm_mem=row(norm_mem[i]),
            w_q=w_q[i].astype(bf16), w_kv=w_kv[i].astype(bf16), w_o=w_o[i].astype(bf16),
            norm_mlp=row(norm_mlp[i]), w_up=w_up[i].astype(bf16), w_down=w_down[i].astype(bf16),
        ))
    p = dict(layers=layers, norm_final=row(norm_final), ssd_expand=_ssd_expand_matrix())
    tables = _dft_tables()
    return (_trunk(x_prompt, mem_prompt, p, tables), _trunk(x_sample, mem_sample, p, tables))
```

```python
import functools
import math

import jax
import jax.numpy as jnp
from jax import lax
from jax.experimental import pallas as pl
from jax.experimental.pallas import tpu as pltpu

f32 = jnp.float32
bf16 = jnp.bfloat16

D_MODEL = 1024
DEPTH = 2
N_MEM = 256
HY_WIDTH = 512
HY_CONV = 3
HY_BANDS = 16
HY_ORDER = 64
HY_DECAY_TARGET = 1e-2
HY_FAST_PCT = 0.3
HY_SLOW_PCT = 1.5
SSM_WIDTH = 512
SSM_HEAD_DIM = 64
SSM_HEADS = 8
SSM_GROUPS = 2
SSM_STATE = 128
SSM_CONV = 5
SSM_CHUNK = 128
BC_WIDTH = SSM_GROUPS * SSM_STATE
XBC_WIDTH = SSM_WIDTH + 2 * BC_WIDTH
HY_IN = 3 * HY_WIDTH
MAIN_IN = HY_IN + SSM_WIDTH + XBC_WIDTH
DT_WIDTH = 2 * SSM_HEADS
XA_HEADS = 4
XA_HEAD_DIM = D_MODEL // XA_HEADS
D_FF = 4 * D_MODEL
EPS = 1e-5

LANES = 128
SUBLANES = 8
HALO = SUBLANES
SUB_LEN = 512
SUB_FFT = 2 * SUB_LEN
HY_SLABS = HY_WIDTH // LANES
ROW_TILE = 1024
SSD_ROWS = 2048
PW_ROWS = 32
VMEM_LIMIT = 56 * 1024 * 1024


def _cparams(*sem):
    return pltpu.CompilerParams(dimension_semantics=sem, vmem_limit_bytes=VMEM_LIMIT)


def _rms(x, w):
    return x * lax.rsqrt(jnp.mean(x * x, axis=-1, keepdims=True) + EPS) * w


def _dot(a, b):
    return jnp.dot(a, b, preferred_element_type=f32)


def _dot_hi(a, b):
    return jnp.dot(a, b, preferred_element_type=f32, precision=lax.Precision.HIGHEST)


def _dot_3pass(a, b):
    a_hi = a.astype(bf16)
    a_lo = (a - a_hi.astype(f32)).astype(bf16)
    b_hi = b.astype(bf16)
    b_lo = (b - b_hi.astype(f32)).astype(bf16)
    return _dot(a_hi, b_hi) + _dot(a_hi, b_lo) + _dot(a_lo, b_hi)


def _dot_nt(a, b):
    return lax.dot_general(a, b, (((1,), (1,)), ((), ())), preferred_element_type=f32)


def _dot_tn(a, b):
    return lax.dot_general(a, b, (((0,), (0,)), ((), ())), preferred_element_type=f32)


def _split3(x):
    hi = x.astype(bf16)
    r1 = x - hi.astype(f32)
    mid = r1.astype(bf16)
    lo = (r1 - mid.astype(f32)).astype(bf16)
    return hi, mid, lo


def _silu(x):
    return x / (1.0 + jnp.exp(-x))


def _softplus(x):
    return jnp.maximum(x, 0.0) + jnp.log(1.0 + jnp.exp(-jnp.abs(x)))


def _const_spec(shape):
    nd = len(shape)
    return pl.BlockSpec(shape, lambda *_: (0,) * nd)


def _inproj_kernel(xp_ref, x_ref, xn_ref, nw_ref, w_ref, wdtt_ref, hcw_ref, hcb_ref,
                   scw_ref, scb_ref, x0_ref, z_ref, gate_ref, xbc_ref, dtt_ref, pbuf):
    tm = x_ref.shape[1]
    i = pl.program_id(1)
    nw = nw_ref[...]
    xp = jnp.where(i > 0, _rms(xp_ref[0], nw), 0.0)
    xn = jnp.where(i < pl.num_programs(1) - 1, _rms(xn_ref[0], nw), 0.0)
    xm = _rms(x_ref[0], nw)
    xe = jnp.concatenate([xp, xm, xn], axis=0).astype(bf16)
    xmb = xe[HALO:HALO + tm]

    def conv_slabs(buf, col0, cw_ref, cb_ref, taps, wcol0):
        p = _dot(xe, w_ref[:, col0:col0 + HY_WIDTH])
        for s in range(HY_SLABS):
            pbuf[buf, s] = p[:, s * LANES:(s + 1) * LANES]
        pad = taps // 2
        outs = []
        for s in range(HY_SLABS):
            cs = slice(wcol0 + s * LANES, wcol0 + (s + 1) * LANES)
            acc = cb_ref[:, cs]
            for k in range(taps):
                acc = acc + pbuf[buf, s, pl.ds(HALO + k - pad, tm), :] * cw_ref[k:k + 1, cs]
            outs.append(acc)
        return outs

    for s, blk in enumerate(conv_slabs(0, 0, hcw_ref, hcb_ref, HY_CONV, 0)):
        x0_ref[0, :, s * LANES:(s + 1) * LANES] = blk.astype(x0_ref.dtype)
    for s, blk in enumerate(conv_slabs(1, HY_WIDTH, hcw_ref, hcb_ref, HY_CONV, HY_WIDTH)):
        z_ref[0, s] = blk
    for s, blk in enumerate(conv_slabs(0, 2 * HY_WIDTH, hcw_ref, hcb_ref, HY_CONV, 2 * HY_WIDTH)):
        z_ref[0, s] = z_ref[0, s] * blk
    gate_ref[0] = _dot(xmb, w_ref[:, HY_IN:HY_IN + SSM_WIDTH]).astype(gate_ref.dtype)
    o2 = HY_IN + SSM_WIDTH
    for c in range(XBC_WIDTH // HY_WIDTH):
        for s, blk in enumerate(conv_slabs(1 - c % 2, o2 + c * HY_WIDTH, scw_ref, scb_ref, SSM_CONV, c * HY_WIDTH)):
            xbc_ref[0, :, c * HY_WIDTH + s * LANES:c * HY_WIDTH + (s + 1) * LANES] = _silu(blk).astype(xbc_ref.dtype)
    dtt_ref[0] = _dot_nt(wdtt_ref[...], xmb)


def _inproj(x, nw, w_main, w_dtt, hcw, hcb, scw, scb):
    b, l, d = x.shape
    tm = ROW_TILE
    nt = l // tm
    hb = tm // HALO
    grid = (b, nt)
    out_shape = (
        jax.ShapeDtypeStruct((b, l, HY_WIDTH), bf16),
        jax.ShapeDtypeStruct((b, HY_SLABS, l, LANES), f32),
        jax.ShapeDtypeStruct((b, l, SSM_WIDTH), bf16),
        jax.ShapeDtypeStruct((b, l, XBC_WIDTH), bf16),
        jax.ShapeDtypeStruct((b, DT_WIDTH, l), f32),
    )
    in_specs = [
        pl.BlockSpec((1, HALO, d), lambda bi, i: (bi, jnp.maximum(i * hb - 1, 0), 0)),
        pl.BlockSpec((1, tm, d), lambda bi, i: (bi, i, 0)),
        pl.BlockSpec((1, HALO, d), lambda bi, i: (bi, jnp.minimum((i + 1) * hb, l // HALO - 1), 0)),
        _const_spec(nw.shape), _const_spec(w_main.shape),
        _const_spec(w_dtt.shape), _const_spec(hcw.shape), _const_spec(hcb.shape),
        _const_spec(scw.shape), _const_spec(scb.shape),
    ]
    out_specs = (
        pl.BlockSpec((1, tm, HY_WIDTH), lambda bi, i: (bi, i, 0)),
        pl.BlockSpec((1, HY_SLABS, tm, LANES), lambda bi, i: (bi, 0, i, 0)),
        pl.BlockSpec((1, tm, SSM_WIDTH), lambda bi, i: (bi, i, 0)),
        pl.BlockSpec((1, tm, XBC_WIDTH), lambda bi, i: (bi, i, 0)),
        pl.BlockSpec((1, DT_WIDTH, tm), lambda bi, i: (bi, 0, i)),
    )
    return pl.pallas_call(
        _inproj_kernel, grid=grid, in_specs=in_specs, out_specs=out_specs, out_shape=out_shape,
        scratch_shapes=[pltpu.VMEM((2, HY_SLABS, tm + 2 * HALO, LANES), f32)],
        compiler_params=_cparams("parallel", "parallel"), name="inproj",
    )(x, x, x, nw, w_main, w_dtt, hcw, hcb, scw, scb)


def _spectra_kernel(fw1t_ref, fw1c_ref, fw1s_ref, fb1_ref, fw2_ref, fb2_ref, fw3_ref, fb3_ref,
                    freq_ref, tc_ref, ts_ref, kre_ref, kim_ref, *, seq_len, phases):
    d = pl.program_id(0)
    lane = lax.broadcasted_iota(jnp.int32, (1, 2 * SUB_LEN), 1)
    m = lane % SUB_LEN
    pos = jnp.abs(jnp.where(lane >= SUB_LEN, d - phases * m, d + phases * m)).astype(f32)
    t = pos * (1.0 / (seq_len - 1))
    w = pos * (2.0 * math.pi / seq_len)
    band = lax.broadcasted_iota(jnp.int32, (HY_BANDS, 1), 0).astype(f32)
    fband = 1e-4 + band * ((HY_BANDS - 1 - 1e-4) / (HY_BANDS - 1))
    ang = fband * w
    fr = freq_ref[...]
    pre = fw1t_ref[...] * t + _dot_hi(fw1c_ref[...], jnp.cos(ang)) - _dot_hi(fw1s_ref[...], jnp.sin(ang))
    h = jnp.sin(fr * (pre + fb1_ref[...]))
    h = jnp.sin(fr * (_dot_hi(fw2_ref[...], h) + fb2_ref[...]))
    o = _dot_3pass(fw3_ref[...], h) + fb3_ref[...]
    max_decay = math.log(HY_DECAY_TARGET) / HY_FAST_PCT
    min_decay = math.log(HY_DECAY_TARGET) / HY_SLOW_PCT
    ch = lax.broadcasted_iota(jnp.int32, (HY_WIDTH, 1), 0).astype(f32)
    deltas = jnp.abs(min_decay + ch * ((max_decay - min_decay) / (HY_WIDTH - 1)))
    decay = jnp.exp(-deltas * t)
    fwd = o[:HY_WIDTH] * decay
    bwd = o[HY_WIDTH:] * decay
    lag0 = m[:, :SUB_LEN] == 0
    gp = fwd[:, :SUB_LEN]
    gm = jnp.where(lag0, 0.0, bwd[:, SUB_LEN:])
    kre_ref[0, 0] = _dot_nt(tc_ref[...], (gp + gm).astype(bf16))
    kim_ref[0, 0] = _dot_nt(ts_ref[...], (gp - gm).astype(bf16))
    gp = jnp.where(jnp.logical_and(lag0, d > 0), bwd[:, SUB_LEN:], fwd[:, SUB_LEN:])
    gm = jnp.where(lag0, 0.0, bwd[:, :SUB_LEN])
    kre_ref[1, 0] = _dot_nt(tc_ref[...], (gp + gm).astype(bf16))
    kim_ref[1, 0] = _dot_nt(ts_ref[...], (gp - gm).astype(bf16))


def _filter_spectra(seq_len, fw1, fb1, fw2, fb2, fw3, fb3, freq, tc, ts):
    phases = seq_len // SUB_LEN
    col = lambda a: a.reshape(-1, 1)
    args = (col(fw1[0]), fw1[1:1 + HY_BANDS].T, fw1[1 + HY_BANDS:].T, col(fb1), fw2.T, col(fb2), fw3.T, col(fb3),
            col(freq), tc, ts)
    out_shape = (jax.ShapeDtypeStruct((2, phases, SUB_LEN, HY_WIDTH), f32),) * 2
    spec = pl.BlockSpec((2, 1, SUB_LEN, HY_WIDTH), lambda k: (0, k, 0, 0))
    return pl.pallas_call(
        functools.partial(_spectra_kernel, seq_len=seq_len, phases=phases),
        grid=(phases,), in_specs=[_const_spec(a.shape) for a in args], out_specs=(spec, spec),
        out_shape=out_shape, compiler_params=_cparams("parallel"), name="filter_spectra",
    )(*args)


def _dft_tables():
    f = jnp.arange(SUB_LEN, dtype=jnp.int32)[:, None]
    n = jnp.arange(SUB_LEN, dtype=jnp.int32)[None, :]
    ang = (((2 * f + 1) * n) % (2 * SUB_FFT)).astype(f32) * (math.pi / SUB_FFT)
    c, s = jnp.cos(ang), jnp.sin(ang)
    tc, ts = c.astype(bf16), (-s).astype(bf16)
    scale = 2.0 / SUB_FFT
    ic = (scale * c).T.astype(bf16)
    isn = (-scale * s).T.astype(bf16)
    return tc, ts, ic, isn


def _longconv_kernel(z_ref, kre_ref, kim_ref, tc_ref, ts_ref, ic_ref, is_ref, o_ref,
                     zre, zim, wre, wim, *, phases):
    slabs = z_ref.shape[1]
    for q in range(phases):
        zq = jnp.concatenate(
            [z_ref[0, s, pl.ds(q, SUB_LEN, stride=phases), :] for s in range(slabs)], axis=1).astype(bf16)
        zre[q] = _dot(tc_ref[...], zq)
        zim[q] = _dot(ts_ref[...], zq)

    def pointwise(p, slot):
        for r in range(SUB_LEN // PW_ROWS):
            rs = slice(r * PW_ROWS, (r + 1) * PW_ROWS)
            are = jnp.zeros((PW_ROWS, zre.shape[2]), f32)
            aim = jnp.zeros((PW_ROWS, zre.shape[2]), f32)
            for q in range(phases):
                kd = jnp.where(p >= q, p - q, phases + q - p)
                kr, ki = kre_ref[kd, rs, :], kim_ref[kd, rs, :]
                xr, xi = zre[q, rs, :], zim[q, rs, :]
                are = are + (kr * xr - ki * xi)
                aim = aim + (kr * xi + ki * xr)
            wre[slot, rs, :] = are.astype(bf16)
            wim[slot, rs, :] = aim.astype(bf16)

    def inverse(p, slot):
        yp = _dot(ic_ref[...], wre[slot]) + _dot(is_ref[...], wim[slot])
        for s in range(slabs):
            o_ref[0, s, pl.ds(p, SUB_LEN, stride=phases), :] = yp[:, s * LANES:(s + 1) * LANES]

    pointwise(0, 0)

    def body(p, carry):
        pointwise(p, p % 2)
        inverse(p - 1, (p - 1) % 2)
        return carry

    lax.fori_loop(1, phases, body, 0)
    inverse(phases - 1, (phases - 1) % 2)


def _longconv(z4, kre, kim, tcf, tsf, ic, isn):
    b, slabs, l, _ = z4.shape
    phases = l // SUB_LEN
    sb = 2
    cb = sb * LANES
    zspec = pl.BlockSpec((1, sb, l, LANES), lambda j, bi: (bi, j, 0, 0))
    kspec = pl.BlockSpec((2 * phases, SUB_LEN, cb), lambda j, bi: (0, 0, j), pipeline_mode=pl.Buffered(1))
    tspec = _const_spec(tcf.shape)
    return pl.pallas_call(
        functools.partial(_longconv_kernel, phases=phases),
        grid=(slabs // sb, b),
        in_specs=[zspec, kspec, kspec, tspec, tspec, tspec, tspec], out_specs=zspec,
        out_shape=jax.ShapeDtypeStruct(z4.shape, f32),
        scratch_shapes=[pltpu.VMEM((phases, SUB_LEN, cb), f32), pltpu.VMEM((phases, SUB_LEN, cb), f32),
                        pltpu.VMEM((2, SUB_LEN, cb), bf16), pltpu.VMEM((2, SUB_LEN, cb), bf16)],
        compiler_params=_cparams("parallel", "parallel"), name="longconv",
    )(z4, kre.reshape(2 * phases, SUB_LEN, HY_WIDTH), kim.reshape(2 * phases, SUB_LEN, HY_WIDTH),
      tcf, tsf, ic, isn)


def _ssd_kernel(*refs, reverse, final):
    if final:
        xbc_ref, dtt_ref, dtb_ref, al_ref, ex_ref, yf_ref, gate_ref, dskip_ref, nw_ref, y_ref, h_ref = refs
    else:
        xbc_ref, dtt_ref, dtb_ref, al_ref, ex_ref, y_ref, h_ref = refs
    q = SSM_CHUNK
    nch = xbc_ref.shape[1] // q
    nh = SSM_HEADS
    col0 = nh if reverse else 0
    log2e = 1.0 / math.log(2.0)

    @pl.when(pl.program_id(1) == 0)
    def _():
        h_ref[...] = jnp.zeros_like(h_ref)

    ii = lax.broadcasted_iota(jnp.int32, (q, q), 0)
    jj = lax.broadcasted_iota(jnp.int32, (q, q), 1)
    mask = (ii <= jj) if reverse else (ii >= jj)
    m_row = jnp.where((ii >= jj) if reverse else (ii <= jj), 1.0, 0.0).astype(bf16)
    hp = (nh // SSM_GROUPS) * SSM_HEAD_DIM
    first_head = lax.broadcasted_iota(jnp.int32, (q, LANES), 1) < SSM_HEAD_DIM

    dt_all = _softplus(dtt_ref[0, col0:col0 + nh, :] + dtb_ref[...])
    da_all = dt_all * (-jnp.exp(al_ref[...]))
    stack = lambda a: jnp.concatenate([a[:, c * q:(c + 1) * q] for c in range(nch)], axis=0)
    dt = stack(dt_all)
    cs2 = sum(_dot(part, m_row) for part in _split3(stack(da_all))) * log2e
    edge = cs2[:, 0:1] if reverse else cs2[:, q - 1:q]
    e = jnp.exp2(cs2)
    dd = dt * jnp.exp2(edge - cs2)
    rt = cs2 - jnp.log2(dt)
    e_hi = e.astype(bf16).astype(f32)
    dd_hi = dd.astype(bf16).astype(f32)
    pad = jnp.zeros((q - 5 * nh, q), f32)

    h = h_ref[...]
    for ci in range(nch):
        c = (nch - 1 - ci) if reverse else ci
        rows = slice(c * q, (c + 1) * q)
        hr = slice(c * nh, (c + 1) * nh)
        xsb = xbc_ref[0, rows, 0:SSM_WIDTH]
        bm = xbc_ref[0, rows, SSM_WIDTH:SSM_WIDTH + BC_WIDTH]
        cm = xbc_ref[0, rows, SSM_WIDTH + BC_WIDTH:XBC_WIDTH]
        xs = xsb.astype(f32)
        cols = jnp.concatenate([cs2[hr], e_hi[hr], (e - e_hi)[hr], dd_hi[hr], (dd - dd_hi)[hr], pad], axis=0).T
        wide = _dot(cols.astype(bf16), ex_ref[...])
        e_x, dd_x = wide[:, :SSM_WIDTH], wide[:, SSM_WIDTH:]
        xdd = (xs * dd_x).astype(bf16)
        hb = h.astype(bf16)
        ys, sts = [], []
        cstack = jnp.concatenate([cm[:, :SSM_STATE], cm[:, SSM_STATE:]], axis=0)
        bstack = jnp.concatenate([bm[:, :SSM_STATE], bm[:, SSM_STATE:]], axis=0)
        cb_all = _dot_nt(cstack, bstack)
        zh = jnp.zeros((SSM_STATE, hp), bf16)
        h_blocks = jnp.concatenate([jnp.concatenate([hb[:, :hp], zh], axis=1),
                                    jnp.concatenate([zh, hb[:, hp:]], axis=1)], axis=0)
        yoff_all = _dot(cm, h_blocks) * e_x
        for g in range(SSM_GROUPS):
            bg = bm[:, g * SSM_STATE:(g + 1) * SSM_STATE]
            gl = slice(g * hp, (g + 1) * hp)
            cb = cb_all[g * q:(g + 1) * q, g * q:(g + 1) * q]
            yoff = yoff_all[:, gl]
            sts.append(_dot_tn(bg, xdd[:, gl]))
            yd = []
            for pr in range(nh // SSM_GROUPS // 2):
                ms = []
                for hd in (g * (nh // SSM_GROUPS) + 2 * pr, g * (nh // SSM_GROUPS) + 2 * pr + 1):
                    diff = cols[:, hd:hd + 1] - rt[c * nh + hd:c * nh + hd + 1, :]
                    ms.append((cb * jnp.exp2(jnp.where(mask, diff, -1e30))).astype(bf16))
                xp = xsb[:, g * hp + pr * LANES:g * hp + (pr + 1) * LANES]
                zero = jnp.zeros_like(xp)
                rhs = jnp.concatenate([jnp.where(first_head, xp, zero), jnp.where(first_head, zero, xp)], axis=0)
                yd.append(_dot(jnp.concatenate(ms, axis=1), rhs))
            ys.append(jnp.concatenate(yd, axis=1) + yoff)
        y = jnp.concatenate(ys, axis=1)
        edge_x = e_x[0:1, :] if reverse else e_x[q - 1:q, :]
        h = h * edge_x + jnp.concatenate(sts, axis=1)
        if final:
            tot = yf_ref[0, rows, :].astype(f32) + y + xs * dskip_ref[...]
            gated = tot * _silu(gate_ref[0, rows, :].astype(f32))
            y_ref[0, rows, :] = _rms(gated, nw_ref[...]).astype(y_ref.dtype)
        else:
            y_ref[0, rows, :] = y.astype(y_ref.dtype)
    h_ref[...] = h


def _ssd(xbc, dtt, dtb, alog, expand, *, reverse, final_args=None):
    b, l, _ = xbc.shape
    tr = SSD_ROWS
    nt = l // tr
    final = final_args is not None
    d = 1 if reverse else 0

    def blk(i):
        return (nt - 1 - i) if reverse else i

    row3 = lambda w: pl.BlockSpec((1, tr, w), lambda bi, i: (bi, blk(i), 0))
    small = (dtb[d].reshape(SSM_HEADS, 1), alog[d].reshape(SSM_HEADS, 1), expand)
    args = [xbc, dtt, *small]
    in_specs = [row3(XBC_WIDTH), pl.BlockSpec((1, DT_WIDTH, tr), lambda bi, i: (bi, 0, blk(i))),
                *[_const_spec(a.shape) for a in small]]
    if final:
        yf, gate, dskip, nw = final_args
        args += [yf, gate, dskip, nw]
        in_specs += [row3(SSM_WIDTH), row3(SSM_WIDTH), _const_spec(dskip.shape), _const_spec(nw.shape)]
    return pl.pallas_call(
        functools.partial(_ssd_kernel, reverse=reverse, final=final),
        grid=(b, nt), in_specs=in_specs, out_specs=row3(SSM_WIDTH),
        out_shape=jax.ShapeDtypeStruct((b, l, SSM_WIDTH), bf16),
        scratch_shapes=[pltpu.VMEM((SSM_STATE, SSM_WIDTH), f32)],
        compiler_params=_cparams("parallel", "arbitrary"), name="ssd_bwd" if reverse else "ssd_fwd",
    )(*args)


def _ssd_expand_matrix():
    row = jnp.arange(SSM_CHUNK, dtype=jnp.int32)[:, None]
    lane = jnp.arange(2 * SSM_WIDTH, dtype=jnp.int32)[None, :]
    head = (lane % SSM_WIDTH) // SSM_HEAD_DIM
    quantity = lane // SSM_WIDTH
    base = SSM_HEADS + 2 * SSM_HEADS * quantity
    hit = jnp.logical_or(row == base + head, row == base + SSM_HEADS + head)
    return hit.astype(bf16)


def _kv_kernel(mem_ref, nw_ref, wkv_ref, k_ref, v_ref):
    mn = _rms(mem_ref[0], nw_ref[...]).astype(bf16)
    k_ref[0] = _dot(mn, wkv_ref[:, :D_MODEL]).astype(bf16)
    v_ref[0] = _dot(mn, wkv_ref[:, D_MODEL:]).astype(bf16)


def _kv_proj(mem, nw, wkv):
    b, m, d = mem.shape
    spec = pl.BlockSpec((1, m, d), lambda bi: (bi, 0, 0))
    return pl.pallas_call(
        _kv_kernel, grid=(b,), in_specs=[spec, _const_spec(nw.shape), _const_spec(wkv.shape)],
        out_specs=(spec, spec), out_shape=(jax.ShapeDtypeStruct((b, m, d), bf16),) * 2,
        compiler_params=_cparams("parallel"), name="kv_proj",
    )(mem, nw, wkv)


def _mix_attn_kernel(x_ref, x0_ref, z_ref, zc_ref, ys_ref, hb_ref, hn_ref, wout_ref, nx_ref, wq_ref,
                     k_ref, v_ref, wo_ref, o_ref):
    z = jnp.concatenate([z_ref[0, s] for s in range(HY_SLABS)], axis=1)
    zc = jnp.concatenate([zc_ref[0, s] for s in range(HY_SLABS)], axis=1)
    y_hy = _rms(x0_ref[0].astype(f32) * (zc + z * hb_ref[...]), hn_ref[...]).astype(bf16)
    x1 = x_ref[0] + _dot(jnp.concatenate([y_hy, ys_ref[0]], axis=1), wout_ref[...])
    xn = _rms(x1, nx_ref[...]).astype(bf16)
    qa = (_dot(xn, wq_ref[...]) * (XA_HEAD_DIM ** -0.5)).astype(bf16)
    heads = []
    for h in range(XA_HEADS):
        hs = slice(h * XA_HEAD_DIM, (h + 1) * XA_HEAD_DIM)
        s = _dot_nt(qa[:, hs], k_ref[0, :, hs])
        e = jnp.exp(s - jnp.max(s, axis=-1, keepdims=True))
        den = jnp.sum(e, axis=-1, keepdims=True)
        heads.append(_dot(e.astype(bf16), v_ref[0, :, hs]) / den)
    o = jnp.concatenate(heads, axis=1).astype(bf16)
    o_ref[0] = x1 + _dot(o, wo_ref[...])


def _mix_attn(x, x0, z4, zc4, ys, hbias, hnorm, wout, nx, wq, k, v, wo):
    b, l, d = x.shape
    tm = ROW_TILE
    row = lambda w: pl.BlockSpec((1, tm, w), lambda bi, i: (bi, i, 0))
    slab = pl.BlockSpec((1, HY_SLABS, tm, LANES), lambda bi, i: (bi, 0, i, 0))
    memspec = pl.BlockSpec((1, N_MEM, d), lambda bi, i: (bi, 0, 0))
    consts = (hbias, hnorm, wout, nx, wq)
    return pl.pallas_call(
        _mix_attn_kernel, grid=(b, l // tm),
        in_specs=[row(d), row(HY_WIDTH), slab, slab, row(SSM_WIDTH), *[_const_spec(a.shape) for a in consts],
                  memspec, memspec, _const_spec(wo.shape)],
        out_specs=row(d), out_shape=jax.ShapeDtypeStruct((b, l, d), f32),
        compiler_params=_cparams("parallel", "parallel"), name="mix_attn",
    )(x, x0, z4, zc4, ys, *consts, k, v, wo)


def _mlp_kernel(x_ref, nw_ref, wup_ref, wdn_ref, fw_ref, o_ref, *, final):
    x = x_ref[0]
    xn = _rms(x, nw_ref[...]).astype(bf16)
    acc = x
    for c in range(D_FF // D_MODEL):
        cs = slice(c * D_MODEL, (c + 1) * D_MODEL)
        hdn = jnp.maximum(_dot(xn, wup_ref[:, cs]), 0.0)
        acc = acc + _dot((hdn * hdn).astype(bf16), wdn_ref[cs, :])
    o_ref[0] = _rms(acc, fw_ref[...]) if final else acc


def _mlp(x, nw, wup, wdn, fw, final):
    b, l, d = x.shape
    tm = ROW_TILE
    row = pl.BlockSpec((1, tm, d), lambda bi, i: (bi, i, 0))
    wspec = lambda a: pl.BlockSpec(a.shape, lambda bi, i: (0, 0), pipeline_mode=pl.Buffered(1))
    return pl.pallas_call(
        functools.partial(_mlp_kernel, final=final), grid=(b, l // tm),
        in_specs=[row, _const_spec(nw.shape), wspec(wup), wspec(wdn), _const_spec(fw.shape)],
        out_specs=row, out_shape=jax.ShapeDtypeStruct((b, l, d), f32),
        compiler_params=_cparams("parallel", "parallel"), name="mlp",
    )(x, nw, wup, wdn, fw)


def _trunk(x, mem, p, tables):
    tc, ts, ic, isn = tables
    seq_len = x.shape[1]
    for i in range(DEPTH):
        lp = p["layers"][i]
        x0, z4, gate, xbc, dtt = _inproj(x, lp["norm_mix"], lp["w_main"], lp["w_dtt"],
                                         lp["hy_conv_w"], lp["hy_conv_b"], lp["ssm_conv_w"], lp["ssm_conv_b"])
        kre, kim = _filter_spectra(seq_len, lp["hy_fw1"], lp["hy_fb1"], lp["hy_fw2"], lp["hy_fb2"],
                                   lp["hy_fw3"], lp["hy_fb3"], lp["hy_sin_freq"], tc, ts)
        zc4 = _longconv(z4, kre, kim, tc, ts, ic, isn)
        yf = _ssd(xbc, dtt, lp["ssm_dt_bias"], lp["ssm_A_log"], p["ssd_expand"], reverse=False)
        ys = _ssd(xbc, dtt, lp["ssm_dt_bias"], lp["ssm_A_log"], p["ssd_expand"], reverse=True,
                  final_args=(yf, gate, lp["ssm_D_wide"], lp["ssm_norm"]))
        k, v = _kv_proj(mem, lp["norm_mem"], lp["w_kv"])
        x = _mix_attn(x, x0, z4, zc4, ys, lp["hy_bias"], lp["hy_norm"], lp["w_out"], lp["norm_xattn"],
                      lp["w_q"], k, v, lp["w_o"])
        x = _mlp(x, lp["norm_mlp"], lp["w_up"], lp["w_down"], p["norm_final"], final=(i == DEPTH - 1))
    return x


def kernel(x_prompt, x_sample, mem_prompt, mem_sample, norm_mix, w_in, hy_conv_w, hy_conv_b, hy_fw1, hy_fb1, hy_fw2, hy_fb2, hy_fw3, hy_fb3, hy_sin_freq, hy_bias, hy_norm, ssm_conv_w, ssm_conv_b, ssm_dt_bias, ssm_A_log, ssm_D, ssm_norm, w_out, norm_xattn, norm_mem, w_q, w_kv, w_o, norm_mlp, w_up, w_down, norm_final):
    row = lambda a: a.reshape(1, -1)
    layers = []
    for i in range(DEPTH):
        wi = w_in[i].astype(bf16)
        layers.append(dict(
            norm_mix=row(norm_mix[i]), w_main=wi[:, :MAIN_IN], w_dtt=wi[:, MAIN_IN:].T,
            hy_conv_w=hy_conv_w[i], hy_conv_b=row(hy_conv_b[i]),
            ssm_conv_w=ssm_conv_w[i], ssm_conv_b=row(ssm_conv_b[i]),
            hy_fw1=hy_fw1[i], hy_fb1=hy_fb1[i], hy_fw2=hy_fw2[i], hy_fb2=hy_fb2[i], hy_fw3=hy_fw3[i],
            hy_fb3=hy_fb3[i], hy_sin_freq=hy_sin_freq[i], hy_bias=row(hy_bias[i]), hy_norm=row(hy_norm[i]),
            ssm_dt_bias=ssm_dt_bias[i], ssm_A_log=ssm_A_log[i],
            ssm_D_wide=row(jnp.repeat(ssm_D[i], SSM_HEAD_DIM)), ssm_norm=row(ssm_norm[i]),
            w_out=w_out[i].astype(bf16), norm_xattn=row(norm_xattn[i]), norm_mem=row(norm_mem[i]),
            w_q=w_q[i].astype(bf16), w_kv=w_kv[i].astype(bf16), w_o=w_o[i].astype(bf16),
            norm_mlp=row(norm_mlp[i]), w_up=w_up[i].astype(bf16), w_down=w_down[i].astype(bf16),
        ))
    p = dict(layers=layers, norm_final=row(norm_final), ssd_expand=_ssd_expand_matrix())
    tables = _dft_tables()
    return (_trunk(x_prompt, mem_prompt, p, tables), _trunk(x_sample, mem_sample, p, tables))
```

```python
import functools
import math

import jax
import jax.numpy as jnp
from jax import lax
from jax.experimental import pallas as pl
from jax.experimental.pallas import tpu as pltpu

f32 = jnp.float32
bf16 = jnp.bfloat16

D_MODEL = 1024
DEPTH = 2
N_MEM = 256
HY_WIDTH = 512
HY_CONV = 3
HY_BANDS = 16
HY_ORDER = 64
HY_DECAY_TARGET = 1e-2
HY_FAST_PCT = 0.3
HY_SLOW_PCT = 1.5
SSM_WIDTH = 512
SSM_HEAD_DIM = 64
SSM_HEADS = 8
SSM_GROUPS = 2
SSM_STATE = 128
SSM_CONV = 5
SSM_CHUNK = 128
BC_WIDTH = SSM_GROUPS * SSM_STATE
XBC_WIDTH = SSM_WIDTH + 2 * BC_WIDTH
HY_IN = 3 * HY_WIDTH
MAIN_IN = HY_IN + SSM_WIDTH + XBC_WIDTH
DT_WIDTH = 2 * SSM_HEADS
XA_HEADS = 4
XA_HEAD_DIM = D_MODEL // XA_HEADS
D_FF = 4 * D_MODEL
EPS = 1e-5

LANES = 128
SUBLANES = 8
HALO = SUBLANES
SUB_LEN = 512
SUB_FFT = 2 * SUB_LEN
HY_SLABS = HY_WIDTH // LANES
ROW_TILE = 1024
PW_ROWS = 32
VMEM_LIMIT = 56 * 1024 * 1024


def _cparams(*sem):
    return pltpu.CompilerParams(dimension_semantics=sem, vmem_limit_bytes=VMEM_LIMIT)


def _rms(x, w):
    return x * lax.rsqrt(jnp.mean(x * x, axis=-1, keepdims=True) + EPS) * w


def _dot(a, b):
    return jnp.dot(a, b, preferred_element_type=f32)


def _dot_hi(a, b):
    return jnp.dot(a, b, preferred_element_type=f32, precision=lax.Precision.HIGHEST)


def _dot_3pass(a, b):
    a_hi = a.astype(bf16)
    a_lo = (a - a_hi.astype(f32)).astype(bf16)
    b_hi = b.astype(bf16)
    b_lo = (b - b_hi.astype(f32)).astype(bf16)
    return _dot(a_hi, b_hi) + _dot(a_hi, b_lo) + _dot(a_lo, b_hi)


def _dot_nt(a, b):
    return lax.dot_general(a, b, (((1,), (1,)), ((), ())), preferred_element_type=f32)


def _dot_tn(a, b):
    return lax.dot_general(a, b, (((0,), (0,)), ((), ())), preferred_element_type=f32)


def _split3(x):
    hi = x.astype(bf16)
    r1 = x - hi.astype(f32)
    mid = r1.astype(bf16)
    lo = (r1 - mid.astype(f32)).astype(bf16)
    return hi, mid, lo


def _silu(x):
    return x / (1.0 + jnp.exp(-x))


def _softplus(x):
    return jnp.maximum(x, 0.0) + jnp.log(1.0 + jnp.exp(-jnp.abs(x)))


def _const_spec(shape):
    nd = len(shape)
    return pl.BlockSpec(shape, lambda *_: (0,) * nd)


def _inproj_kernel(xp_ref, x_ref, xn_ref, nw_ref, w_ref, wdtt_ref, hcw_ref, hcb_ref,
                   scw_ref, scb_ref, x0_ref, z_ref, gate_ref, xbc_ref, dtt_ref, pbuf):
    tm = x_ref.shape[1]
    i = pl.program_id(1)
    nw = nw_ref[...]
    xp = jnp.where(i > 0, _rms(xp_ref[0], nw), 0.0)
    xn = jnp.where(i < pl.num_programs(1) - 1, _rms(xn_ref[0], nw), 0.0)
    xm = _rms(x_ref[0], nw)
    xe = jnp.concatenate([xp, xm, xn], axis=0).astype(bf16)
    xmb = xe[HALO:HALO + tm]

    def conv_slabs(buf, col0, cw_ref, cb_ref, taps, wcol0):
        p = _dot(xe, w_ref[:, col0:col0 + HY_WIDTH])
        for s in range(HY_SLABS):
            pbuf[buf, s] = p[:, s * LANES:(s + 1) * LANES]
        pad = taps // 2
        outs = []
        for s in range(HY_SLABS):
            cs = slice(wcol0 + s * LANES, wcol0 + (s + 1) * LANES)
            acc = cb_ref[:, cs]
            for k in range(taps):
                acc = acc + pbuf[buf, s, pl.ds(HALO + k - pad, tm), :] * cw_ref[k:k + 1, cs]
            outs.append(acc)
        return outs

    for s, blk in enumerate(conv_slabs(0, 0, hcw_ref, hcb_ref, HY_CONV, 0)):
        x0_ref[0, :, s * LANES:(s + 1) * LANES] = blk.astype(x0_ref.dtype)
    for s, blk in enumerate(conv_slabs(1, HY_WIDTH, hcw_ref, hcb_ref, HY_CONV, HY_WIDTH)):
        z_ref[0, s] = blk
    for s, blk in enumerate(conv_slabs(0, 2 * HY_WIDTH, hcw_ref, hcb_ref, HY_CONV, 2 * HY_WIDTH)):
        z_ref[0, s] = z_ref[0, s] * blk
    gate_ref[0] = _dot(xmb, w_ref[:, HY_IN:HY_IN + SSM_WIDTH]).astype(gate_ref.dtype)
    o2 = HY_IN + SSM_WIDTH
    for c in range(XBC_WIDTH // HY_WIDTH):
        for s, blk in enumerate(conv_slabs(1 - c % 2, o2 + c * HY_WIDTH, scw_ref, scb_ref, SSM_CONV, c * HY_WIDTH)):
            xbc_ref[0, :, c * HY_WIDTH + s * LANES:c * HY_WIDTH + (s + 1) * LANES] = _silu(blk).astype(xbc_ref.dtype)
    dtt_ref[0] = _dot_nt(wdtt_ref[...], xmb)


def _inproj(x, nw, w_main, w_dtt, hcw, hcb, scw, scb):
    b, l, d = x.shape
    tm = ROW_TILE
    nt = l // tm
    hb = tm // HALO
    grid = (b, nt)
    out_shape = (
        jax.ShapeDtypeStruct((b, l, HY_WIDTH), bf16),
        jax.ShapeDtypeStruct((b, HY_SLABS, l, LANES), f32),
        jax.ShapeDtypeStruct((b, l, SSM_WIDTH), bf16),
        jax.ShapeDtypeStruct((b, l, XBC_WIDTH), bf16),
        jax.ShapeDtypeStruct((b, DT_WIDTH, l), f32),
    )
    in_specs = [
        pl.BlockSpec((1, HALO, d), lambda bi, i: (bi, jnp.maximum(i * hb - 1, 0), 0)),
        pl.BlockSpec((1, tm, d), lambda bi, i: (bi, i, 0)),
        pl.BlockSpec((1, HALO, d), lambda bi, i: (bi, jnp.minimum((i + 1) * hb, l // HALO - 1), 0)),
        _const_spec(nw.shape), _const_spec(w_main.shape),
        _const_spec(w_dtt.shape), _const_spec(hcw.shape), _const_spec(hcb.shape),
        _const_spec(scw.shape), _const_spec(scb.shape),
    ]
    out_specs = (
        pl.BlockSpec((1, tm, HY_WIDTH), lambda bi, i: (bi, i, 0)),
        pl.BlockSpec((1, HY_SLABS, tm, LANES), lambda bi, i: (bi, 0, i, 0)),
        pl.BlockSpec((1, tm, SSM_WIDTH), lambda bi, i: (bi, i, 0)),
        pl.BlockSpec((1, tm, XBC_WIDTH), lambda bi, i: (bi, i, 0)),
        pl.BlockSpec((1, DT_WIDTH, tm), lambda bi, i: (bi, 0, i)),
    )
    return pl.pallas_call(
        _inproj_kernel, grid=grid, in_specs=in_specs, out_specs=out_specs, out_shape=out_shape,
        scratch_shapes=[pltpu.VMEM((2, HY_SLABS, tm + 2 * HALO, LANES), f32)],
        compiler_params=_cparams("parallel", "parallel"), name="inproj",
    )(x, x, x, nw, w_main, w_dtt, hcw, hcb, scw, scb)


def _spectra_kernel(fw1t_ref, fw1c_ref, fw1s_ref, fb1_ref, fw2_ref, fb2_ref, fw3_ref, fb3_ref,
                    freq_ref, tc_ref, ts_ref, kre_ref, kim_ref, *, seq_len, phases):
    d = pl.program_id(0)
    lane = lax.broadcasted_iota(jnp.int32, (1, 2 * SUB_LEN), 1)
    m = lane % SUB_LEN
    pos = jnp.abs(jnp.where(lane >= SUB_LEN, d - phases * m, d + phases * m)).astype(f32)
    t = pos * (1.0 / (seq_len - 1))
    w = pos * (2.0 * math.pi / seq_len)
    band = lax.broadcasted_iota(jnp.int32, (HY_BANDS, 1), 0).astype(f32)
    fband = 1e-4 + band * ((HY_BANDS - 1 - 1e-4) / (HY_BANDS - 1))
    ang = fband * w
    fr = freq_ref[...]
    pre = fw1t_ref[...] * t + _dot_hi(fw1c_ref[...], jnp.cos(ang)) - _dot_hi(fw1s_ref[...], jnp.sin(ang))
    h = jnp.sin(fr * (pre + fb1_ref[...]))
    h = jnp.sin(fr * (_dot_hi(fw2_ref[...], h) + fb2_ref[...]))
    o = _dot_3pass(fw3_ref[...], h) + fb3_ref[...]
    max_decay = math.log(HY_DECAY_TARGET) / HY_FAST_PCT
    min_decay = math.log(HY_DECAY_TARGET) / HY_SLOW_PCT
    ch = lax.broadcasted_iota(jnp.int32, (HY_WIDTH, 1), 0).astype(f32)
    deltas = jnp.abs(min_decay + ch * ((max_decay - min_decay) / (HY_WIDTH - 1)))
    decay = jnp.exp(-deltas * t)
    fwd = o[:HY_WIDTH] * decay
    bwd = o[HY_WIDTH:] * decay
    lag0 = m[:, :SUB_LEN] == 0
    gp = fwd[:, :SUB_LEN]
    gm = jnp.where(lag0, 0.0, bwd[:, SUB_LEN:])
    kre_ref[0, 0] = _dot_nt(tc_ref[...], (gp + gm).astype(bf16))
    kim_ref[0, 0] = _dot_nt(ts_ref[...], (gp - gm).astype(bf16))
    gp = jnp.where(jnp.logical_and(lag0, d > 0), bwd[:, SUB_LEN:], fwd[:, SUB_LEN:])
    gm = jnp.where(lag0, 0.0, bwd[:, :SUB_LEN])
    kre_ref[1, 0] = _dot_nt(tc_ref[...], (gp + gm).astype(bf16))
    kim_ref[1, 0] = _dot_nt(ts_ref[...], (gp - gm).astype(bf16))


def _filter_spectra(seq_len, fw1, fb1, fw2, fb2, fw3, fb3, freq, tc, ts):
    phases = seq_len // SUB_LEN
    col = lambda a: a.reshape(-1, 1)
    args = (col(fw1[0]), fw1[1:1 + HY_BANDS].T, fw1[1 + HY_BANDS:].T, col(fb1), fw2.T, col(fb2), fw3.T, col(fb3),
            col(freq), tc, ts)
    out_shape = (jax.ShapeDtypeStruct((2, phases, SUB_LEN, HY_WIDTH), f32),) * 2
    spec = pl.BlockSpec((2, 1, SUB_LEN, HY_WIDTH), lambda k: (0, k, 0, 0))
    return pl.pallas_call(
        functools.partial(_spectra_kernel, seq_len=seq_len, phases=phases),
        grid=(phases,), in_specs=[_const_spec(a.shape) for a in args], out_specs=(spec, spec),
        out_shape=out_shape, compiler_params=_cparams("parallel"), name="filter_spectra",
    )(*args)


def _dft_tables():
    f = jnp.arange(SUB_LEN, dtype=jnp.int32)[:, None]
    n = jnp.arange(SUB_LEN, dtype=jnp.int32)[None, :]
    ang = (((2 * f + 1) * n) % (2 * SUB_FFT)).astype(f32) * (math.pi / SUB_FFT)
    c, s = jnp.cos(ang), jnp.sin(ang)
    tc, ts = c.astype(bf16), (-s).astype(bf16)
    scale = 2.0 / SUB_FFT
    ic = (scale * c).T.astype(bf16)
    isn = (-scale * s).T.astype(bf16)
    return tc, ts, ic, isn


def _longconv_kernel(z_ref, kre_ref, kim_ref, tc_ref, ts_ref, ic_ref, is_ref, o_ref,
                     zre, zim, wre, wim, *, phases):
    slabs = z_ref.shape[1]
    for q in range(phases):
        zq = jnp.concatenate(
            [z_ref[0, s, pl.ds(q, SUB_LEN, stride=phases), :] for s in range(slabs)], axis=1).astype(bf16)
        zre[q] = _dot(tc_ref[...], zq)
        zim[q] = _dot(ts_ref[...], zq)

    def pointwise(p, slot):
        for r in range(SUB_LEN // PW_ROWS):
            rs = slice(r * PW_ROWS, (r + 1) * PW_ROWS)
            are = jnp.zeros((PW_ROWS, zre.shape[2]), f32)
            aim = jnp.zeros((PW_ROWS, zre.shape[2]), f32)
            for q in range(phases):
                kd = jnp.where(p >= q, p - q, phases + q - p)
                kr, ki = kre_ref[kd, rs, :], kim_ref[kd, rs, :]
                xr, xi = zre[q, rs, :], zim[q, rs, :]
                are = are + (kr * xr - ki * xi)
                aim = aim + (kr * xi + ki * xr)
            wre[slot, rs, :] = are.astype(bf16)
            wim[slot, rs, :] = aim.astype(bf16)

    def inverse(p, slot):
        yp = _dot(ic_ref[...], wre[slot]) + _dot(is_ref[...], wim[slot])
        for s in range(slabs):
            o_ref[0, s, pl.ds(p, SUB_LEN, stride=phases), :] = yp[:, s * LANES:(s + 1) * LANES]

    pointwise(0, 0)

    def body(p, carry):
        pointwise(p, p % 2)
        inverse(p - 1, (p - 1) % 2)
        return carry

    lax.fori_loop(1, phases, body, 0)
    inverse(phases - 1, (phases - 1) % 2)


def _longconv(z4, kre, kim, tcf, tsf, ic, isn):
    b, slabs, l, _ = z4.shape
    phases = l // SUB_LEN
    sb = 2
    cb = sb * LANES
    zspec = pl.BlockSpec((1, sb, l, LANES), lambda j, bi: (bi, j, 0, 0))
    kspec = pl.BlockSpec((2 * phases, SUB_LEN, cb), lambda j, bi: (0, 0, j), pipeline_mode=pl.Buffered(1))
    tspec = _const_spec(tcf.shape)
    return pl.pallas_call(
        functools.partial(_longconv_kernel, phases=phases),
        grid=(slabs // sb, b),
        in_specs=[zspec, kspec, kspec, tspec, tspec, tspec, tspec], out_specs=zspec,
        out_shape=jax.ShapeDtypeStruct(z4.shape, f32),
        scratch_shapes=[pltpu.VMEM((phases, SUB_LEN, cb), f32), pltpu.VMEM((phases, SUB_LEN, cb), f32),
                        pltpu.VMEM((2, SUB_LEN, cb), bf16), pltpu.VMEM((2, SUB_LEN, cb), bf16)],
        compiler_params=_cparams("parallel", "parallel"), name="longconv",
    )(z4, kre.reshape(2 * phases, SUB_LEN, HY_WIDTH), kim.reshape(2 * phases, SUB_LEN, HY_WIDTH),
      tcf, tsf, ic, isn)


SSD_ROWS_PER_DIR = 5 * SSM_HEADS
SSD_UNROLL = 8


def _ssd_bidir_kernel(xbc_ref, dtt_ref, dtb_ref, al_ref, ex_ref, gate_ref, dskip_ref, nw_ref, y_ref,
                      smat, rtmat, yacc, stash, hf_ref, hb_ref):
    q = SSM_CHUNK
    nh = SSM_HEADS
    n = xbc_ref.shape[1] // q
    half = n // 2
    rows_dir = SSD_ROWS_PER_DIR
    hp = (nh // SSM_GROUPS) * SSM_HEAD_DIM
    log2e = 1.0 / math.log(2.0)

    ii = lax.broadcasted_iota(jnp.int32, (q, q), 0)
    jj = lax.broadcasted_iota(jnp.int32, (q, q), 1)
    lower, upper = ii >= jj, ii <= jj
    first_head = lax.broadcasted_iota(jnp.int32, (q, LANES), 1) < SSM_HEAD_DIM

    dt16 = _softplus(dtt_ref[0] + dtb_ref[...])
    da16 = dt16 * (-jnp.exp(al_ref[...]))
    stack = lambda a: jnp.concatenate([a[:, c * q:(c + 1) * q] for c in range(n)], axis=0)
    smat[:, 2 * rows_dir:, :] = jnp.zeros((n, q - 2 * rows_dir, q), f32)
    for d in range(2):
        m_sum = jnp.where(upper if d == 0 else lower, 1.0, 0.0).astype(bf16)
        dt = stack(dt16[d * nh:(d + 1) * nh])
        cs2 = sum(_dot(part, m_sum) for part in _split3(stack(da16[d * nh:(d + 1) * nh]))) * log2e
        edge = cs2[:, q - 1:q] if d == 0 else cs2[:, 0:1]
        e = jnp.exp2(cs2)
        dd = dt * jnp.exp2(edge - cs2)
        e_hi = e.astype(bf16).astype(f32)
        dd_hi = dd.astype(bf16).astype(f32)
        for j, a in enumerate((cs2, e_hi, e - e_hi, dd_hi, dd - dd_hi)):
            smat[:, d * rows_dir + j * nh:d * rows_dir + (j + 1) * nh, :] = a.reshape(n, nh, q)
        rtmat[:, d * nh:(d + 1) * nh, :] = (cs2 - jnp.log2(dt)).reshape(n, nh, q)
    hf_ref[...] = jnp.zeros_like(hf_ref)
    hb_ref[...] = jnp.zeros_like(hb_ref)

    def chunk_rows(c):
        return pl.ds(pl.multiple_of(c * q, q), q)

    def in_chunk(c, d, h_ref):
        rows = chunk_rows(c)
        xsb = xbc_ref[0, rows, 0:SSM_WIDTH]
        bm = xbc_ref[0, rows, SSM_WIDTH:SSM_WIDTH + BC_WIDTH]
        cm = xbc_ref[0, rows, SSM_WIDTH + BC_WIDTH:XBC_WIDTH]
        xs = xsb.astype(f32)
        cols = smat[c].T
        wide = _dot(cols.astype(bf16), ex_ref[...])
        e_f, dd_f = wide[:, 0:SSM_WIDTH], wide[:, SSM_WIDTH:2 * SSM_WIDTH]
        e_b, dd_b = wide[:, 2 * SSM_WIDTH:3 * SSM_WIDTH], wide[:, 3 * SSM_WIDTH:]
        xdd_f = (xs * dd_f).astype(bf16)
        xdd_b = (xs * dd_b).astype(bf16)
        rt = rtmat[c]
        cstack = jnp.concatenate([cm[:, :SSM_STATE], cm[:, SSM_STATE:]], axis=0)
        bstack = jnp.concatenate([bm[:, :SSM_STATE], bm[:, SSM_STATE:]], axis=0)
        cb_all = _dot_nt(cstack, bstack)
        ys, sts_f, sts_b = [], [], []
        for g in range(SSM_GROUPS):
            bg = bm[:, g * SSM_STATE:(g + 1) * SSM_STATE]
            gl = slice(g * hp, (g + 1) * hp)
            cb = cb_all[g * q:(g + 1) * q, g * q:(g + 1) * q]
            st = _dot_tn(bg, jnp.concatenate([xdd_f[:, gl], xdd_b[:, gl]], axis=1))
            sts_f.append(st[:, :hp])
            sts_b.append(st[:, hp:])
            for pr in range(nh // SSM_GROUPS // 2):
                ms = []
                for hd in (g * (nh // SSM_GROUPS) + 2 * pr, g * (nh // SSM_GROUPS) + 2 * pr + 1):
                    df = cols[:, hd:hd + 1] - rt[hd:hd + 1, :]
                    db = cols[:, rows_dir + hd:rows_dir + hd + 1] - rt[nh + hd:nh + hd + 1, :]
                    lm = jnp.exp2(jnp.where(lower, df, -1e30)) + jnp.exp2(jnp.where(upper, db, -1e30))
                    ms.append((cb * lm).astype(bf16))
                xp = xsb[:, g * hp + pr * LANES:g * hp + (pr + 1) * LANES]
                zero = jnp.zeros_like(xp)
                rhs = jnp.concatenate([jnp.where(first_head, xp, zero), jnp.where(first_head, zero, xp)], axis=0)
                ys.append(_dot(jnp.concatenate(ms, axis=1), rhs))
        st = (jnp.concatenate(sts_f, axis=1), jnp.concatenate(sts_b, axis=1))
        yoff = carried(c, h_ref, (e_f, e_b)[d], st[d], (q - 1, 0)[d])
        yacc[rows, :] = jnp.concatenate(ys, axis=1) + xs * dskip_ref[...] + yoff
        return st[1 - d]

    def expand_e(c, d):
        cols = smat[c].T
        return _dot(cols.astype(bf16), ex_ref[:, 2 * d * SSM_WIDTH:(2 * d + 1) * SSM_WIDTH])

    def carried(c, h_ref, e_x, st, edge_row):
        cm = xbc_ref[0, chunk_rows(c), SSM_WIDTH + BC_WIDTH:XBC_WIDTH]
        h = h_ref[...]
        hb = h.astype(bf16)
        zh = jnp.zeros((SSM_STATE, hp), bf16)
        h_blocks = jnp.concatenate([jnp.concatenate([hb[:, :hp], zh], axis=1),
                                    jnp.concatenate([zh, hb[:, hp:]], axis=1)], axis=0)
        h_ref[...] = h * e_x[edge_row:edge_row + 1, :] + st
        return _dot(cm, h_blocks) * e_x

    def finish(c, y):
        rows = chunk_rows(c)
        gated = y * _silu(gate_ref[0, rows, :].astype(f32))
        y_ref[0, rows, :] = _rms(gated, nw_ref[...]).astype(y_ref.dtype)

    def first_half(k, carry):
        m = n - 1 - k
        stash[k] = in_chunk(k, 0, hf_ref)
        stash[m] = in_chunk(m, 1, hb_ref)
        return carry

    def second_half(k, carry):
        m = n - 1 - k
        finish(k, yacc[chunk_rows(k), :] + carried(k, hf_ref, expand_e(k, 0), stash[k], q - 1))
        finish(m, yacc[chunk_rows(m), :] + carried(m, hb_ref, expand_e(m, 1), stash[m], 0))
        return carry

    lax.fori_loop(0, half, first_half, 0, unroll=SSD_UNROLL)
    lax.fori_loop(half, n, second_half, 0, unroll=SSD_UNROLL)


def _ssd_bidir(xbc, dtt, gate, dtb, alog, expand, dskip, nw):
    b, l, _ = xbc.shape
    n = l // SSM_CHUNK
    seq = lambda w: pl.BlockSpec((1, l, w), lambda bi: (bi, 0, 0))
    small = (dtb.reshape(DT_WIDTH, 1), alog.reshape(DT_WIDTH, 1), expand)
    return pl.pallas_call(
        _ssd_bidir_kernel, grid=(b,),
        in_specs=[seq(XBC_WIDTH), pl.BlockSpec((1, DT_WIDTH, l), lambda bi: (bi, 0, 0)),
                  *[_const_spec(a.shape) for a in small], seq(SSM_WIDTH),
                  _const_spec(dskip.shape), _const_spec(nw.shape)],
        out_specs=seq(SSM_WIDTH), out_shape=jax.ShapeDtypeStruct((b, l, SSM_WIDTH), bf16),
        scratch_shapes=[pltpu.VMEM((n, SSM_CHUNK, SSM_CHUNK), f32),
                        pltpu.VMEM((n, DT_WIDTH, SSM_CHUNK), f32),
                        pltpu.VMEM((l, SSM_WIDTH), f32),
                        pltpu.VMEM((n, SSM_STATE, SSM_WIDTH), f32),
                        pltpu.VMEM((SSM_STATE, SSM_WIDTH), f32), pltpu.VMEM((SSM_STATE, SSM_WIDTH), f32)],
        compiler_params=_cparams("parallel"), name="ssd_bidir",
    )(xbc, dtt, *small, gate, dskip, nw)


def _ssd_bidir_expand_matrix():
    row = jnp.arange(SSM_CHUNK, dtype=jnp.int32)[:, None]
    lane = jnp.arange(4 * SSM_WIDTH, dtype=jnp.int32)[None, :]
    head = (lane % SSM_WIDTH) // SSM_HEAD_DIM
    quantity = (lane // SSM_WIDTH) % 2
    direction = lane // (2 * SSM_WIDTH)
    base = direction * SSD_ROWS_PER_DIR + SSM_HEADS + 2 * SSM_HEADS * quantity
    hit = jnp.logical_or(row == base + head, row == base + SSM_HEADS + head)
    return hit.astype(bf16)


def _kv_kernel(mem_ref, nw_ref, wkv_ref, k_ref, v_ref):
    mn = _rms(mem_ref[0], nw_ref[...]).astype(bf16)
    k_ref[0] = _dot(mn, wkv_ref[:, :D_MODEL]).astype(bf16)
    v_ref[0] = _dot(mn, wkv_ref[:, D_MODEL:]).astype(bf16)


def _kv_proj(mem, nw, wkv):
    b, m, d = mem.shape
    spec = pl.BlockSpec((1, m, d), lambda bi: (bi, 0, 0))
    return pl.pallas_call(
        _kv_kernel, grid=(b,), in_specs=[spec, _const_spec(nw.shape), _const_spec(wkv.shape)],
        out_specs=(spec, spec), out_shape=(jax.ShapeDtypeStruct((b, m, d), bf16),) * 2,
        compiler_params=_cparams("parallel"), name="kv_proj",
    )(mem, nw, wkv)


def _mix_attn_kernel(x_ref, x0_ref, z_ref, zc_ref, ys_ref, hb_ref, hn_ref, wout_ref, nx_ref, wq_ref,
                     k_ref, v_ref, wo_ref, o_ref):
    z = jnp.concatenate([z_ref[0, s] for s in range(HY_SLABS)], axis=1)
    zc = jnp.concatenate([zc_ref[0, s] for s in range(HY_SLABS)], axis=1)
    y_hy = _rms(x0_ref[0].astype(f32) * (zc + z * hb_ref[...]), hn_ref[...]).astype(bf16)
    x1 = x_ref[0] + _dot(jnp.concatenate([y_hy, ys_ref[0]], axis=1), wout_ref[...])
    xn = _rms(x1, nx_ref[...]).astype(bf16)
    qa = (_dot(xn, wq_ref[...]) * (XA_HEAD_DIM ** -0.5)).astype(bf16)
    heads = []
    for h in range(XA_HEADS):
        hs = slice(h * XA_HEAD_DIM, (h + 1) * XA_HEAD_DIM)
        s = _dot_nt(qa[:, hs], k_ref[0, :, hs])
        e = jnp.exp(s - jnp.max(s, axis=-1, keepdims=True))
        den = jnp.sum(e, axis=-1, keepdims=True)
        heads.append(_dot(e.astype(bf16), v_ref[0, :, hs]) / den)
    o = jnp.concatenate(heads, axis=1).astype(bf16)
    o_ref[0] = x1 + _dot(o, wo_ref[...])


def _mix_attn(x, x0, z4, zc4, ys, hbias, hnorm, wout, nx, wq, k, v, wo):
    b, l, d = x.shape
    tm = ROW_TILE
    row = lambda w: pl.BlockSpec((1, tm, w), lambda bi, i: (bi, i, 0))
    slab = pl.BlockSpec((1, HY_SLABS, tm, LANES), lambda bi, i: (bi, 0, i, 0))
    memspec = pl.BlockSpec((1, N_MEM, d), lambda bi, i: (bi, 0, 0))
    consts = (hbias, hnorm, wout, nx, wq)
    return pl.pallas_call(
        _mix_attn_kernel, grid=(b, l // tm),
        in_specs=[row(d), row(HY_WIDTH), slab, slab, row(SSM_WIDTH), *[_const_spec(a.shape) for a in consts],
                  memspec, memspec, _const_spec(wo.shape)],
        out_specs=row(d), out_shape=jax.ShapeDtypeStruct((b, l, d), f32),
        compiler_params=_cparams("parallel", "parallel"), name="mix_attn",
    )(x, x0, z4, zc4, ys, *consts, k, v, wo)


def _mlp_kernel(x_ref, nw_ref, wup_ref, wdn_ref, fw_ref, o_ref, *, final):
    x = x_ref[0]
    xn = _rms(x, nw_ref[...]).astype(bf16)
    acc = x
    for c in range(D_FF // D_MODEL):
        cs = slice(c * D_MODEL, (c + 1) * D_MODEL)
        hdn = jnp.maximum(_dot(xn, wup_ref[:, cs]), 0.0)
        acc = acc + _dot((hdn * hdn).astype(bf16), wdn_ref[cs, :])
    o_ref[0] = _rms(acc, fw_ref[...]) if final else acc


def _mlp(x, nw, wup, wdn, fw, final):
    b, l, d = x.shape
    tm = ROW_TILE
    row = pl.BlockSpec((1, tm, d), lambda bi, i: (bi, i, 0))
    wspec = lambda a: pl.BlockSpec(a.shape, lambda bi, i: (0, 0), pipeline_mode=pl.Buffered(1))
    return pl.pallas_call(
        functools.partial(_mlp_kernel, final=final), grid=(b, l // tm),
        in_specs=[row, _const_spec(nw.shape), wspec(wup), wspec(wdn), _const_spec(fw.shape)],
        out_specs=row, out_shape=jax.ShapeDtypeStruct((b, l, d), f32),
        compiler_params=_cparams("parallel", "parallel"), name="mlp",
    )(x, nw, wup, wdn, fw)


def _trunk(x, mem, p, tables):
    tc, ts, ic, isn = tables
    seq_len = x.shape[1]
    for i in range(DEPTH):
        lp = p["layers"][i]
        x0, z4, gate, xbc, dtt = _inproj(x, lp["norm_mix"], lp["w_main"], lp["w_dtt"],
                                         lp["hy_conv_w"], lp["hy_conv_b"], lp["ssm_conv_w"], lp["ssm_conv_b"])
        kre, kim = _filter_spectra(seq_len, lp["hy_fw1"], lp["hy_fb1"], lp["hy_fw2"], lp["hy_fb2"],
                                   lp["hy_fw3"], lp["hy_fb3"], lp["hy_sin_freq"], tc, ts)
        zc4 = _longconv(z4, kre, kim, tc, ts, ic, isn)
        ys = _ssd_bidir(xbc, dtt, gate, lp["ssm_dt_bias"], lp["ssm_A_log"], p["ssd_expand"],
                        lp["ssm_D_wide"], lp["ssm_norm"])
        k, v = _kv_proj(mem, lp["norm_mem"], lp["w_kv"])
        x = _mix_attn(x, x0, z4, zc4, ys, lp["hy_bias"], lp["hy_norm"], lp["w_out"], lp["norm_xattn"],
                      lp["w_q"], k, v, lp["w_o"])
        x = _mlp(x, lp["norm_mlp"], lp["w_up"], lp["w_down"], p["norm_final"], final=(i == DEPTH - 1))
    return x


def kernel(x_prompt, x_sample, mem_prompt, mem_sample, norm_mix, w_in, hy_conv_w, hy_conv_b, hy_fw1, hy_fb1, hy_fw2, hy_fb2, hy_fw3, hy_fb3, hy_sin_freq, hy_bias, hy_norm, ssm_conv_w, ssm_conv_b, ssm_dt_bias, ssm_A_log, ssm_D, ssm_norm, w_out, norm_xattn, norm_mem, w_q, w_kv, w_o, norm_mlp, w_up, w_down, norm_final):
    row = lambda a: a.reshape(1, -1)
    layers = []
    for i in range(DEPTH):
        wi = w_in[i].astype(bf16)
        layers.append(dict(
            norm_mix=row(norm_mix[i]), w_main=wi[:, :MAIN_IN], w_dtt=wi[:, MAIN_IN:].T,
            hy_conv_w=hy_conv_w[i], hy_conv_b=row(hy_conv_b[i]),
            ssm_conv_w=ssm_conv_w[i], ssm_conv_b=row(ssm_conv_b[i]),
            hy_fw1=hy_fw1[i], hy_fb1=hy_fb1[i], hy_fw2=hy_fw2[i], hy_fb2=hy_fb2[i], hy_fw3=hy_fw3[i],
            hy_fb3=hy_fb3[i], hy_sin_freq=hy_sin_freq[i], hy_bias=row(hy_bias[i]), hy_norm=row(hy_norm[i]),
            ssm_dt_bias=ssm_dt_bias[i], ssm_A_log=ssm_A_log[i],
            ssm_D_wide=row(jnp.repeat(ssm_D[i], SSM_HEAD_DIM)), ssm_norm=row(ssm_norm[i]),
            w_out=w_out[i].astype(bf16), norm_xattn=row(norm_xattn[i]), norm_mem=row(norm_mem[i]),
            w_q=w_q[i].astype(bf16), w_kv=w_kv[i].astype(bf16), w_o=w_o[i].astype(bf16),
            norm_mlp=row(norm_mlp[i]), w_up=w_up[i].astype(bf16), w_down=w_down[i].astype(bf16),
        ))
    p = dict(layers=layers, norm_final=row(norm_final), ssd_expand=_ssd_bidir_expand_matrix())
    tables = _dft_tables()
    return (_trunk(x_prompt, mem_prompt, p, tables), _trunk(x_sample, mem_sample, p, tables))
```

```python
import functools
import math

import jax
import jax.numpy as jnp
from jax import lax
from jax.experimental import pallas as pl
from jax.experimental.pallas import tpu as pltpu

f32 = jnp.float32
bf16 = jnp.bfloat16

D_MODEL = 1024
DEPTH = 2
N_MEM = 256
HY_WIDTH = 512
HY_CONV = 3
HY_BANDS = 16
HY_ORDER = 64
HY_DECAY_TARGET = 1e-2
HY_FAST_PCT = 0.3
HY_SLOW_PCT = 1.5
SSM_WIDTH = 512
SSM_HEAD_DIM = 64
SSM_HEADS = 8
SSM_GROUPS = 2
SSM_STATE = 128
SSM_CONV = 5
SSM_CHUNK = 128
BC_WIDTH = SSM_GROUPS * SSM_STATE
XBC_WIDTH = SSM_WIDTH + 2 * BC_WIDTH
HY_IN = 3 * HY_WIDTH
MAIN_IN = HY_IN + SSM_WIDTH + XBC_WIDTH
DT_WIDTH = 2 * SSM_HEADS
XA_HEADS = 4
XA_HEAD_DIM = D_MODEL // XA_HEADS
D_FF = 4 * D_MODEL
EPS = 1e-5

LANES = 128
SUBLANES = 8
HALO = SUBLANES
SUB_LEN = 512
SUB_FFT = 2 * SUB_LEN
HY_SLABS = HY_WIDTH // LANES
ROW_TILE = 1024
PW_ROWS = 32
VMEM_LIMIT = 56 * 1024 * 1024


def _cparams(*sem):
    return pltpu.CompilerParams(dimension_semantics=sem, vmem_limit_bytes=VMEM_LIMIT)


def _rms(x, w):
    return x * lax.rsqrt(jnp.mean(x * x, axis=-1, keepdims=True) + EPS) * w


def _dot(a, b):
    return jnp.dot(a, b, preferred_element_type=f32)


def _dot_hi(a, b):
    return jnp.dot(a, b, preferred_element_type=f32, precision=lax.Precision.HIGHEST)


def _dot_3pass(a, b):
    a_hi = a.astype(bf16)
    a_lo = (a - a_hi.astype(f32)).astype(bf16)
    b_hi = b.astype(bf16)
    b_lo = (b - b_hi.astype(f32)).astype(bf16)
    return _dot(a_hi, b_hi) + _dot(a_hi, b_lo) + _dot(a_lo, b_hi)


def _dot_nt(a, b):
    return lax.dot_general(a, b, (((1,), (1,)), ((), ())), preferred_element_type=f32)


def _dot_tn(a, b):
    return lax.dot_general(a, b, (((0,), (0,)), ((), ())), preferred_element_type=f32)


def _split3(x):
    hi = x.astype(bf16)
    r1 = x - hi.astype(f32)
    mid = r1.astype(bf16)
    lo = (r1 - mid.astype(f32)).astype(bf16)
    return hi, mid, lo


def _silu(x):
    return x / (1.0 + jnp.exp(-x))


def _softplus(x):
    return jnp.maximum(x, 0.0) + jnp.log(1.0 + jnp.exp(-jnp.abs(x)))


def _const_spec(shape):
    nd = len(shape)
    return pl.BlockSpec(shape, lambda *_: (0,) * nd)


def _inproj_kernel(xp_ref, x_ref, xn_ref, nw_ref, w_ref, wdtt_ref, hcw_ref, hcb_ref,
                   scw_ref, scb_ref, x0_ref, z_ref, gate_ref, xbc_ref, dtt_ref, pbuf):
    tm = x_ref.shape[1]
    i = pl.program_id(1)
    nw = nw_ref[...]
    xp = jnp.where(i > 0, _rms(xp_ref[0], nw), 0.0)
    xn = jnp.where(i < pl.num_programs(1) - 1, _rms(xn_ref[0], nw), 0.0)
    xm = _rms(x_ref[0], nw)
    xe = jnp.concatenate([xp, xm, xn], axis=0).astype(bf16)
    xmb = xe[HALO:HALO + tm]

    def conv_slabs(buf, col0, cw_ref, cb_ref, taps, wcol0):
        p = _dot(xe, w_ref[:, col0:col0 + HY_WIDTH])
        for s in range(HY_SLABS):
            pbuf[buf, s] = p[:, s * LANES:(s + 1) * LANES]
        pad = taps // 2
        outs = []
        for s in range(HY_SLABS):
            cs = slice(wcol0 + s * LANES, wcol0 + (s + 1) * LANES)
            acc = cb_ref[:, cs]
            for k in range(taps):
                acc = acc + pbuf[buf, s, pl.ds(HALO + k - pad, tm), :] * cw_ref[k:k + 1, cs]
            outs.append(acc)
        return outs

    for s, blk in enumerate(conv_slabs(0, 0, hcw_ref, hcb_ref, HY_CONV, 0)):
        x0_ref[0, :, s * LANES:(s + 1) * LANES] = blk.astype(x0_ref.dtype)
    for s, blk in enumerate(conv_slabs(1, HY_WIDTH, hcw_ref, hcb_ref, HY_CONV, HY_WIDTH)):
        z_ref[0, s] = blk
    for s, blk in enumerate(conv_slabs(0, 2 * HY_WIDTH, hcw_ref, hcb_ref, HY_CONV, 2 * HY_WIDTH)):
        z_ref[0, s] = z_ref[0, s] * blk
    gate_ref[0] = _dot(xmb, w_ref[:, HY_IN:HY_IN + SSM_WIDTH]).astype(gate_ref.dtype)
    o2 = HY_IN + SSM_WIDTH
    for c in range(XBC_WIDTH // HY_WIDTH):
        for s, blk in enumerate(conv_slabs(1 - c % 2, o2 + c * HY_WIDTH, scw_ref, scb_ref, SSM_CONV, c * HY_WIDTH)):
            xbc_ref[0, :, c * HY_WIDTH + s * LANES:c * HY_WIDTH + (s + 1) * LANES] = _silu(blk).astype(xbc_ref.dtype)
    dtt_ref[0] = _dot_nt(wdtt_ref[...], xmb)


def _inproj(x, nw, w_main, w_dtt, hcw, hcb, scw, scb):
    b, l, d = x.shape
    tm = ROW_TILE
    nt = l // tm
    hb = tm // HALO
    grid = (b, nt)
    out_shape = (
        jax.ShapeDtypeStruct((b, l, HY_WIDTH), bf16),
        jax.ShapeDtypeStruct((b, HY_SLABS, l, LANES), f32),
        jax.ShapeDtypeStruct((b, l, SSM_WIDTH), bf16),
        jax.ShapeDtypeStruct((b, l, XBC_WIDTH), bf16),
        jax.ShapeDtypeStruct((b, DT_WIDTH, l), f32),
    )
    in_specs = [
        pl.BlockSpec((1, HALO, d), lambda bi, i: (bi, jnp.maximum(i * hb - 1, 0), 0)),
        pl.BlockSpec((1, tm, d), lambda bi, i: (bi, i, 0)),
        pl.BlockSpec((1, HALO, d), lambda bi, i: (bi, jnp.minimum((i + 1) * hb, l // HALO - 1), 0)),
        _const_spec(nw.shape), _const_spec(w_main.shape),
        _const_spec(w_dtt.shape), _const_spec(hcw.shape), _const_spec(hcb.shape),
        _const_spec(scw.shape), _const_spec(scb.shape),
    ]
    out_specs = (
        pl.BlockSpec((1, tm, HY_WIDTH), lambda bi, i: (bi, i, 0)),
        pl.BlockSpec((1, HY_SLABS, tm, LANES), lambda bi, i: (bi, 0, i, 0)),
        pl.BlockSpec((1, tm, SSM_WIDTH), lambda bi, i: (bi, i, 0)),
        pl.BlockSpec((1, tm, XBC_WIDTH), lambda bi, i: (bi, i, 0)),
        pl.BlockSpec((1, DT_WIDTH, tm), lambda bi, i: (bi, 0, i)),
    )
    return pl.pallas_call(
        _inproj_kernel, grid=grid, in_specs=in_specs, out_specs=out_specs, out_shape=out_shape,
        scratch_shapes=[pltpu.VMEM((2, HY_SLABS, tm + 2 * HALO, LANES), f32)],
        compiler_params=_cparams("parallel", "parallel"), name="inproj",
    )(x, x, x, nw, w_main, w_dtt, hcw, hcb, scw, scb)


def _spectra_kernel(fw1t_ref, fw1c_ref, fw1s_ref, fb1_ref, fw2_ref, fb2_ref, fw3_ref, fb3_ref,
                    freq_ref, tc_ref, ts_ref, kre_ref, kim_ref, *, seq_len, phases):
    d = pl.program_id(0)
    lane = lax.broadcasted_iota(jnp.int32, (1, 2 * SUB_LEN), 1)
    m = lane % SUB_LEN
    pos = jnp.abs(jnp.where(lane >= SUB_LEN, d - phases * m, d + phases * m)).astype(f32)
    t = pos * (1.0 / (seq_len - 1))
    w = pos * (2.0 * math.pi / seq_len)
    band = lax.broadcasted_iota(jnp.int32, (HY_BANDS, 1), 0).astype(f32)
    fband = 1e-4 + band * ((HY_BANDS - 1 - 1e-4) / (HY_BANDS - 1))
    ang = fband * w
    fr = freq_ref[...]
    pre = fw1t_ref[...] * t + _dot_hi(fw1c_ref[...], jnp.cos(ang)) - _dot_hi(fw1s_ref[...], jnp.sin(ang))
    h = jnp.sin(fr * (pre + fb1_ref[...]))
    h = jnp.sin(fr * (_dot_hi(fw2_ref[...], h) + fb2_ref[...]))
    o = _dot_3pass(fw3_ref[...], h) + fb3_ref[...]
    max_decay = math.log(HY_DECAY_TARGET) / HY_FAST_PCT
    min_decay = math.log(HY_DECAY_TARGET) / HY_SLOW_PCT
    ch = lax.broadcasted_iota(jnp.int32, (HY_WIDTH, 1), 0).astype(f32)
    deltas = jnp.abs(min_decay + ch * ((max_decay - min_decay) / (HY_WIDTH - 1)))
    decay = jnp.exp(-deltas * t)
    fwd = o[:HY_WIDTH] * decay
    bwd = o[HY_WIDTH:] * decay
    lag0 = m[:, :SUB_LEN] == 0
    gp = fwd[:, :SUB_LEN]
    gm = jnp.where(lag0, 0.0, bwd[:, SUB_LEN:])
    kre_ref[0, 0] = _dot_nt(tc_ref[...], (gp + gm).astype(bf16))
    kim_ref[0, 0] = _dot_nt(ts_ref[...], (gp - gm).astype(bf16))
    gp = jnp.where(jnp.logical_and(lag0, d > 0), bwd[:, SUB_LEN:], fwd[:, SUB_LEN:])
    gm = jnp.where(lag0, 0.0, bwd[:, :SUB_LEN])
    kre_ref[1, 0] = _dot_nt(tc_ref[...], (gp + gm).astype(bf16))
    kim_ref[1, 0] = _dot_nt(ts_ref[...], (gp - gm).astype(bf16))


def _filter_spectra(seq_len, fw1, fb1, fw2, fb2, fw3, fb3, freq, tc, ts):
    phases = seq_len // SUB_LEN
    col = lambda a: a.reshape(-1, 1)
    args = (col(fw1[0]), fw1[1:1 + HY_BANDS].T, fw1[1 + HY_BANDS:].T, col(fb1), fw2.T, col(fb2), fw3.T, col(fb3),
            col(freq), tc, ts)
    out_shape = (jax.ShapeDtypeStruct((2, phases, SUB_LEN, HY_WIDTH), f32),) * 2
    spec = pl.BlockSpec((2, 1, SUB_LEN, HY_WIDTH), lambda k: (0, k, 0, 0))
    return pl.pallas_call(
        functools.partial(_spectra_kernel, seq_len=seq_len, phases=phases),
        grid=(phases,), in_specs=[_const_spec(a.shape) for a in args], out_specs=(spec, spec),
        out_shape=out_shape, compiler_params=_cparams("parallel"), name="filter_spectra",
    )(*args)


def _dft_tables():
    f = jnp.arange(SUB_LEN, dtype=jnp.int32)[:, None]
    n = jnp.arange(SUB_LEN, dtype=jnp.int32)[None, :]
    ang = (((2 * f + 1) * n) % (2 * SUB_FFT)).astype(f32) * (math.pi / SUB_FFT)
    c, s = jnp.cos(ang), jnp.sin(ang)
    tc, ts = c.astype(bf16), (-s).astype(bf16)
    scale = 2.0 / SUB_FFT
    ic = (scale * c).T.astype(bf16)
    isn = (-scale * s).T.astype(bf16)
    return tc, ts, ic, isn


def _longconv_kernel(z_ref, kre_ref, kim_ref, tc_ref, ts_ref, ic_ref, is_ref, o_ref,
                     zre, zim, wre, wim, *, phases):
    slabs = z_ref.shape[1]
    for q in range(phases):
        zq = jnp.concatenate(
            [z_ref[0, s, pl.ds(q, SUB_LEN, stride=phases), :] for s in range(slabs)], axis=1).astype(bf16)
        zre[q] = _dot(tc_ref[...], zq)
        zim[q] = _dot(ts_ref[...], zq)

    def pointwise(p, slot):
        for r in range(SUB_LEN // PW_ROWS):
            rs = slice(r * PW_ROWS, (r + 1) * PW_ROWS)
            are = jnp.zeros((PW_ROWS, zre.shape[2]), f32)
            aim = jnp.zeros((PW_ROWS, zre.shape[2]), f32)
            for q in range(phases):
                kd = jnp.where(p >= q, p - q, phases + q - p)
                kr, ki = kre_ref[kd, rs, :], kim_ref[kd, rs, :]
                xr, xi = zre[q, rs, :], zim[q, rs, :]
                are = are + (kr * xr - ki * xi)
                aim = aim + (kr * xi + ki * xr)
            wre[slot, rs, :] = are.astype(bf16)
            wim[slot, rs, :] = aim.astype(bf16)

    def inverse(p, slot):
        yp = _dot(ic_ref[...], wre[slot]) + _dot(is_ref[...], wim[slot])
        for s in range(slabs):
            o_ref[0, s, pl.ds(p, SUB_LEN, stride=phases), :] = yp[:, s * LANES:(s + 1) * LANES]

    pointwise(0, 0)

    def body(p, carry):
        pointwise(p, p % 2)
        inverse(p - 1, (p - 1) % 2)
        return carry

    lax.fori_loop(1, phases, body, 0)
    inverse(phases - 1, (phases - 1) % 2)


def _longconv(z4, kre, kim, tcf, tsf, ic, isn):
    b, slabs, l, _ = z4.shape
    phases = l // SUB_LEN
    sb = 2
    cb = sb * LANES
    zspec = pl.BlockSpec((1, sb, l, LANES), lambda j, bi: (bi, j, 0, 0))
    spectra_bytes = 2 * (2 * phases) * SUB_LEN * cb * 4
    kspec = pl.BlockSpec((2 * phases, SUB_LEN, cb), lambda j, bi: (0, 0, j),
                         pipeline_mode=pl.Buffered(1 if 2 * spectra_bytes > VMEM_LIMIT // 2 else 2))
    tspec = _const_spec(tcf.shape)
    return pl.pallas_call(
        functools.partial(_longconv_kernel, phases=phases),
        grid=(slabs // sb, b),
        in_specs=[zspec, kspec, kspec, tspec, tspec, tspec, tspec], out_specs=zspec,
        out_shape=jax.ShapeDtypeStruct(z4.shape, f32),
        scratch_shapes=[pltpu.VMEM((phases, SUB_LEN, cb), f32), pltpu.VMEM((phases, SUB_LEN, cb), f32),
                        pltpu.VMEM((2, SUB_LEN, cb), bf16), pltpu.VMEM((2, SUB_LEN, cb), bf16)],
        compiler_params=_cparams("parallel", "parallel"), name="longconv",
    )(z4, kre.reshape(2 * phases, SUB_LEN, HY_WIDTH), kim.reshape(2 * phases, SUB_LEN, HY_WIDTH),
      tcf, tsf, ic, isn)


SSD_ROWS_PER_DIR = 5 * SSM_HEADS
SSD_UNROLL = 8


def _ssd_bidir_kernel(xbc_ref, dtt_ref, dtb_ref, al_ref, ex_ref, gate_ref, dskip_ref, nw_ref, y_ref,
                      smat, rtmat, yacc, stash, hf_ref, hb_ref):
    q = SSM_CHUNK
    nh = SSM_HEADS
    n = xbc_ref.shape[1] // q
    half = n // 2
    rows_dir = SSD_ROWS_PER_DIR
    hp = (nh // SSM_GROUPS) * SSM_HEAD_DIM
    log2e = 1.0 / math.log(2.0)

    ii = lax.broadcasted_iota(jnp.int32, (q, q), 0)
    jj = lax.broadcasted_iota(jnp.int32, (q, q), 1)
    lower, upper = ii >= jj, ii <= jj
    first_head = lax.broadcasted_iota(jnp.int32, (q, LANES), 1) < SSM_HEAD_DIM

    dt16 = _softplus(dtt_ref[0] + dtb_ref[...])
    da16 = dt16 * (-jnp.exp(al_ref[...]))
    stack = lambda a: jnp.concatenate([a[:, c * q:(c + 1) * q] for c in range(n)], axis=0)
    smat[:, 2 * rows_dir:, :] = jnp.zeros((n, q - 2 * rows_dir, q), f32)
    for d in range(2):
        m_sum = jnp.where(upper if d == 0 else lower, 1.0, 0.0).astype(bf16)
        dt = stack(dt16[d * nh:(d + 1) * nh])
        cs2 = sum(_dot(part, m_sum) for part in _split3(stack(da16[d * nh:(d + 1) * nh]))) * log2e
        edge = cs2[:, q - 1:q] if d == 0 else cs2[:, 0:1]
        e = jnp.exp2(cs2)
        dd = dt * jnp.exp2(edge - cs2)
        e_hi = e.astype(bf16).astype(f32)
        dd_hi = dd.astype(bf16).astype(f32)
        for j, a in enumerate((cs2, e_hi, e - e_hi, dd_hi, dd - dd_hi)):
            smat[:, d * rows_dir + j * nh:d * rows_dir + (j + 1) * nh, :] = a.reshape(n, nh, q)
        rtmat[:, d * nh:(d + 1) * nh, :] = (cs2 - jnp.log2(dt)).reshape(n, nh, q)
    hf_ref[...] = jnp.zeros_like(hf_ref)
    hb_ref[...] = jnp.zeros_like(hb_ref)

    def chunk_rows(c):
        return pl.ds(pl.multiple_of(c * q, q), q)

    def in_chunk(c, d, h_ref):
        rows = chunk_rows(c)
        xsb = xbc_ref[0, rows, 0:SSM_WIDTH]
        bm = xbc_ref[0, rows, SSM_WIDTH:SSM_WIDTH + BC_WIDTH]
        cm = xbc_ref[0, rows, SSM_WIDTH + BC_WIDTH:XBC_WIDTH]
        xs = xsb.astype(f32)
        cols = smat[c].T
        wide = _dot(cols.astype(bf16), ex_ref[...])
        e_f, dd_f = wide[:, 0:SSM_WIDTH], wide[:, SSM_WIDTH:2 * SSM_WIDTH]
        e_b, dd_b = wide[:, 2 * SSM_WIDTH:3 * SSM_WIDTH], wide[:, 3 * SSM_WIDTH:]
        xdd_f = (xs * dd_f).astype(bf16)
        xdd_b = (xs * dd_b).astype(bf16)
        rt = rtmat[c]
        cstack = jnp.concatenate([cm[:, :SSM_STATE], cm[:, SSM_STATE:]], axis=0)
        bstack = jnp.concatenate([bm[:, :SSM_STATE], bm[:, SSM_STATE:]], axis=0)
        cb_all = _dot_nt(cstack, bstack)
        ys, sts_f, sts_b = [], [], []
        for g in range(SSM_GROUPS):
            bg = bm[:, g * SSM_STATE:(g + 1) * SSM_STATE]
            gl = slice(g * hp, (g + 1) * hp)
            cb = cb_all[g * q:(g + 1) * q, g * q:(g + 1) * q]
            st = _dot_tn(bg, jnp.concatenate([xdd_f[:, gl], xdd_b[:, gl]], axis=1))
            sts_f.append(st[:, :hp])
            sts_b.append(st[:, hp:])
            for pr in range(nh // SSM_GROUPS // 2):
                ms = []
                for hd in (g * (nh // SSM_GROUPS) + 2 * pr, g * (nh // SSM_GROUPS) + 2 * pr + 1):
                    df = cols[:, hd:hd + 1] - rt[hd:hd + 1, :]
                    db = cols[:, rows_dir + hd:rows_dir + hd + 1] - rt[nh + hd:nh + hd + 1, :]
                    lm = jnp.exp2(jnp.where(lower, df, -1e30)) + jnp.exp2(jnp.where(upper, db, -1e30))
                    ms.append((cb * lm).astype(bf16))
                xp = xsb[:, g * hp + pr * LANES:g * hp + (pr + 1) * LANES]
                zero = jnp.zeros_like(xp)
                rhs = jnp.concatenate([jnp.where(first_head, xp, zero), jnp.where(first_head, zero, xp)], axis=0)
                ys.append(_dot(jnp.concatenate(ms, axis=1), rhs))
        st = (jnp.concatenate(sts_f, axis=1), jnp.concatenate(sts_b, axis=1))
        yoff = carried(c, h_ref, (e_f, e_b)[d], st[d], (q - 1, 0)[d])
        yacc[rows, :] = jnp.concatenate(ys, axis=1) + xs * dskip_ref[...] + yoff
        return st[1 - d]

    def expand_e(c, d):
        cols = smat[c].T
        return _dot(cols.astype(bf16), ex_ref[:, 2 * d * SSM_WIDTH:(2 * d + 1) * SSM_WIDTH])

    def carried(c, h_ref, e_x, st, edge_row):
        cm = xbc_ref[0, chunk_rows(c), SSM_WIDTH + BC_WIDTH:XBC_WIDTH]
        h = h_ref[...]
        hb = h.astype(bf16)
        zh = jnp.zeros((SSM_STATE, hp), bf16)
        h_blocks = jnp.concatenate([jnp.concatenate([hb[:, :hp], zh], axis=1),
                                    jnp.concatenate([zh, hb[:, hp:]], axis=1)], axis=0)
        h_ref[...] = h * e_x[edge_row:edge_row + 1, :] + st
        return _dot(cm, h_blocks) * e_x

    def finish(c, y):
        rows = chunk_rows(c)
        gated = y * _silu(gate_ref[0, rows, :].astype(f32))
        y_ref[0, rows, :] = _rms(gated, nw_ref[...]).astype(y_ref.dtype)

    def first_half(k, carry):
        m = n - 1 - k
        stash[k] = in_chunk(k, 0, hf_ref)
        stash[m] = in_chunk(m, 1, hb_ref)
        return carry

    def second_half(k, carry):
        m = n - 1 - k
        finish(k, yacc[chunk_rows(k), :] + carried(k, hf_ref, expand_e(k, 0), stash[k], q - 1))
        finish(m, yacc[chunk_rows(m), :] + carried(m, hb_ref, expand_e(m, 1), stash[m], 0))
        return carry

    lax.fori_loop(0, half, first_half, 0, unroll=SSD_UNROLL)
    lax.fori_loop(half, n, second_half, 0, unroll=SSD_UNROLL)


def _ssd_bidir(xbc, dtt, gate, dtb, alog, expand, dskip, nw):
    b, l, _ = xbc.shape
    n = l // SSM_CHUNK
    seq = lambda w: pl.BlockSpec((1, l, w), lambda bi: (bi, 0, 0))
    small = (dtb.reshape(DT_WIDTH, 1), alog.reshape(DT_WIDTH, 1), expand)
    return pl.pallas_call(
        _ssd_bidir_kernel, grid=(b,),
        in_specs=[seq(XBC_WIDTH), pl.BlockSpec((1, DT_WIDTH, l), lambda bi: (bi, 0, 0)),
                  *[_const_spec(a.shape) for a in small], seq(SSM_WIDTH),
                  _const_spec(dskip.shape), _const_spec(nw.shape)],
        out_specs=seq(SSM_WIDTH), out_shape=jax.ShapeDtypeStruct((b, l, SSM_WIDTH), bf16),
        scratch_shapes=[pltpu.VMEM((n, SSM_CHUNK, SSM_CHUNK), f32),
                        pltpu.VMEM((n, DT_WIDTH, SSM_CHUNK), f32),
                        pltpu.VMEM((l, SSM_WIDTH), f32),
                        pltpu.VMEM((n, SSM_STATE, SSM_WIDTH), f32),
                        pltpu.VMEM((SSM_STATE, SSM_WIDTH), f32), pltpu.VMEM((SSM_STATE, SSM_WIDTH), f32)],
        compiler_params=_cparams("parallel"), name="ssd_bidir",
    )(xbc, dtt, *small, gate, dskip, nw)


def _ssd_bidir_expand_matrix():
    row = jnp.arange(SSM_CHUNK, dtype=jnp.int32)[:, None]
    lane = jnp.arange(4 * SSM_WIDTH, dtype=jnp.int32)[None, :]
    head = (lane % SSM_WIDTH) // SSM_HEAD_DIM
    quantity = (lane // SSM_WIDTH) % 2
    direction = lane // (2 * SSM_WIDTH)
    base = direction * SSD_ROWS_PER_DIR + SSM_HEADS + 2 * SSM_HEADS * quantity
    hit = jnp.logical_or(row == base + head, row == base + SSM_HEADS + head)
    return hit.astype(bf16)


def _kv_kernel(mem_ref, nw_ref, wkv_ref, k_ref, v_ref):
    mn = _rms(mem_ref[0], nw_ref[...]).astype(bf16)
    k_ref[0] = _dot(mn, wkv_ref[:, :D_MODEL]).astype(bf16)
    v_ref[0] = _dot(mn, wkv_ref[:, D_MODEL:]).astype(bf16)


def _kv_proj(mem, nw, wkv):
    b, m, d = mem.shape
    spec = pl.BlockSpec((1, m, d), lambda bi: (bi, 0, 0))
    return pl.pallas_call(
        _kv_kernel, grid=(b,), in_specs=[spec, _const_spec(nw.shape), _const_spec(wkv.shape)],
        out_specs=(spec, spec), out_shape=(jax.ShapeDtypeStruct((b, m, d), bf16),) * 2,
        compiler_params=_cparams("parallel"), name="kv_proj",
    )(mem, nw, wkv)


def _mix_attn_kernel(x_ref, x0_ref, z_ref, zc_ref, ys_ref, hb_ref, hn_ref, wout_ref, nx_ref, wq_ref,
                     k_ref, v_ref, wo_ref, o_ref):
    z = jnp.concatenate([z_ref[0, s] for s in range(HY_SLABS)], axis=1)
    zc = jnp.concatenate([zc_ref[0, s] for s in range(HY_SLABS)], axis=1)
    y_hy = _rms(x0_ref[0].astype(f32) * (zc + z * hb_ref[...]), hn_ref[...]).astype(bf16)
    x1 = x_ref[0] + _dot(jnp.concatenate([y_hy, ys_ref[0]], axis=1), wout_ref[...])
    xn = _rms(x1, nx_ref[...]).astype(bf16)
    qa = (_dot(xn, wq_ref[...]) * (XA_HEAD_DIM ** -0.5)).astype(bf16)
    heads = []
    for h in range(XA_HEADS):
        hs = slice(h * XA_HEAD_DIM, (h + 1) * XA_HEAD_DIM)
        s = _dot_nt(qa[:, hs], k_ref[0, :, hs])
        e = jnp.exp(s - jnp.max(s, axis=-1, keepdims=True))
        den = jnp.sum(e, axis=-1, keepdims=True)
        heads.append(_dot(e.astype(bf16), v_ref[0, :, hs]) / den)
    o = jnp.concatenate(heads, axis=1).astype(bf16)
    o_ref[0] = x1 + _dot(o, wo_ref[...])


def _mix_attn(x, x0, z4, zc4, ys, hbias, hnorm, wout, nx, wq, k, v, wo):
    b, l, d = x.shape
    tm = ROW_TILE
    row = lambda w: pl.BlockSpec((1, tm, w), lambda bi, i: (bi, i, 0))
    slab = pl.BlockSpec((1, HY_SLABS, tm, LANES), lambda bi, i: (bi, 0, i, 0))
    memspec = pl.BlockSpec((1, N_MEM, d), lambda bi, i: (bi, 0, 0))
    consts = (hbias, hnorm, wout, nx, wq)
    return pl.pallas_call(
        _mix_attn_kernel, grid=(b, l // tm),
        in_specs=[row(d), row(HY_WIDTH), slab, slab, row(SSM_WIDTH), *[_const_spec(a.shape) for a in consts],
                  memspec, memspec, _const_spec(wo.shape)],
        out_specs=row(d), out_shape=jax.ShapeDtypeStruct((b, l, d), f32),
        compiler_params=_cparams("parallel", "parallel"), name="mix_attn",
    )(x, x0, z4, zc4, ys, *consts, k, v, wo)


def _mlp_kernel(x_ref, nw_ref, wup_ref, wdn_ref, fw_ref, o_ref, *, final):
    x = x_ref[0]
    xn = _rms(x, nw_ref[...]).astype(bf16)
    acc = x
    for c in range(D_FF // D_MODEL):
        cs = slice(c * D_MODEL, (c + 1) * D_MODEL)
        hdn = jnp.maximum(_dot(xn, wup_ref[:, cs]), 0.0)
        acc = acc + _dot((hdn * hdn).astype(bf16), wdn_ref[cs, :])
    o_ref[0] = _rms(acc, fw_ref[...]) if final else acc


def _mlp(x, nw, wup, wdn, fw, final):
    b, l, d = x.shape
    tm = ROW_TILE
    row = pl.BlockSpec((1, tm, d), lambda bi, i: (bi, i, 0))
    wspec = lambda a: pl.BlockSpec(a.shape, lambda bi, i: (0, 0), pipeline_mode=pl.Buffered(1))
    return pl.pallas_call(
        functools.partial(_mlp_kernel, final=final), grid=(b, l // tm),
        in_specs=[row, _const_spec(nw.shape), wspec(wup), wspec(wdn), _const_spec(fw.shape)],
        out_specs=row, out_shape=jax.ShapeDtypeStruct((b, l, d), f32),
        compiler_params=_cparams("parallel", "parallel"), name="mlp",
    )(x, nw, wup, wdn, fw)


def _trunk(x, mem, p, tables):
    tc, ts, ic, isn = tables
    seq_len = x.shape[1]
    for i in range(DEPTH):
        lp = p["layers"][i]
        x0, z4, gate, xbc, dtt = _inproj(x, lp["norm_mix"], lp["w_main"], lp["w_dtt"],
                                         lp["hy_conv_w"], lp["hy_conv_b"], lp["ssm_conv_w"], lp["ssm_conv_b"])
        kre, kim = _filter_spectra(seq_len, lp["hy_fw1"], lp["hy_fb1"], lp["hy_fw2"], lp["hy_fb2"],
                                   lp["hy_fw3"], lp["hy_fb3"], lp["hy_sin_freq"], tc, ts)
        zc4 = _longconv(z4, kre, kim, tc, ts, ic, isn)
        ys = _ssd_bidir(xbc, dtt, gate, lp["ssm_dt_bias"], lp["ssm_A_log"], p["ssd_expand"],
                        lp["ssm_D_wide"], lp["ssm_norm"])
        k, v = _kv_proj(mem, lp["norm_mem"], lp["w_kv"])
        x = _mix_attn(x, x0, z4, zc4, ys, lp["hy_bias"], lp["hy_norm"], lp["w_out"], lp["norm_xattn"],
                      lp["w_q"], k, v, lp["w_o"])
        x = _mlp(x, lp["norm_mlp"], lp["w_up"], lp["w_down"], p["norm_final"], final=(i == DEPTH - 1))
    return x


def kernel(x_prompt, x_sample, mem_prompt, mem_sample, norm_mix, w_in, hy_conv_w, hy_conv_b, hy_fw1, hy_fb1, hy_fw2, hy_fb2, hy_fw3, hy_fb3, hy_sin_freq, hy_bias, hy_norm, ssm_conv_w, ssm_conv_b, ssm_dt_bias, ssm_A_log, ssm_D, ssm_norm, w_out, norm_xattn, norm_mem, w_q, w_kv, w_o, norm_mlp, w_up, w_down, norm_final):
    row = lambda a: a.reshape(1, -1)
    layers = []
    for i in range(DEPTH):
        wi = w_in[i].astype(bf16)
        layers.append(dict(
            norm_mix=row(norm_mix[i]), w_main=wi, w_dtt=wi[:, MAIN_IN:].T,
            hy_conv_w=hy_conv_w[i], hy_conv_b=row(hy_conv_b[i]),
            ssm_conv_w=ssm_conv_w[i], ssm_conv_b=row(ssm_conv_b[i]),
            hy_fw1=hy_fw1[i], hy_fb1=hy_fb1[i], hy_fw2=hy_fw2[i], hy_fb2=hy_fb2[i], hy_fw3=hy_fw3[i],
            hy_fb3=hy_fb3[i], hy_sin_freq=hy_sin_freq[i], hy_bias=row(hy_bias[i]), hy_norm=row(hy_norm[i]),
            ssm_dt_bias=ssm_dt_bias[i], ssm_A_log=ssm_A_log[i],
            ssm_D_wide=row(jnp.repeat(ssm_D[i], SSM_HEAD_DIM)), ssm_norm=row(ssm_norm[i]),
            w_out=w_out[i].astype(bf16), norm_xattn=row(norm_xattn[i]), norm_mem=row(norm_mem[i]),
            w_q=w_q[i].astype(bf16), w_kv=w_kv[i].astype(bf16), w_o=w_o[i].astype(bf16),
            norm_mlp=row(norm_mlp[i]), w_up=w_up[i].astype(bf16), w_down=w_down[i].astype(bf16),
        ))
    p = dict(layers=layers, norm_final=row(norm_final), ssd_expand=_ssd_bidir_expand_matrix())
    tables = _dft_tables()
    return (_trunk(x_prompt, mem_prompt, p, tables), _trunk(x_sample, mem_sample, p, tables))
```

```python
import functools
import math

import jax
import jax.numpy as jnp
from jax import lax
from jax.experimental import pallas as pl
from jax.experimental.pallas import tpu as pltpu

f32 = jnp.float32
bf16 = jnp.bfloat16

D_MODEL = 1024
DEPTH = 2
N_MEM = 256
HY_WIDTH = 512
HY_CONV = 3
HY_BANDS = 16
HY_DECAY_TARGET = 1e-2
HY_FAST_PCT = 0.3
HY_SLOW_PCT = 1.5
SSM_WIDTH = 512
SSM_HEAD_DIM = 64
SSM_HEADS = 8
SSM_GROUPS = 2
SSM_STATE = 128
SSM_CONV = 5
SSM_CHUNK = 128
BC_WIDTH = SSM_GROUPS * SSM_STATE
XBC_WIDTH = SSM_WIDTH + 2 * BC_WIDTH
HY_IN = 3 * HY_WIDTH
MAIN_IN = HY_IN + SSM_WIDTH + XBC_WIDTH
DT_WIDTH = 2 * SSM_HEADS
XA_HEADS = 4
XA_HEAD_DIM = D_MODEL // XA_HEADS
D_FF = 4 * D_MODEL
EPS = 1e-5

LANES = 128
SUBLANES = 8
HALO = SUBLANES
SUB_LEN = 512
SUB_FFT = 2 * SUB_LEN
HY_SLABS = HY_WIDTH // LANES
ROW_TILE = 1024
PW_ROWS = 32
VMEM_LIMIT = 56 * 1024 * 1024


def _cparams(*sem):
    return pltpu.CompilerParams(dimension_semantics=sem, vmem_limit_bytes=VMEM_LIMIT)


def _rms(x, w=None):
    y = x * lax.rsqrt(jnp.mean(x * x, axis=-1, keepdims=True) + EPS)
    return y if w is None else y * w


def _dot(a, b):
    return jnp.dot(a, b, preferred_element_type=f32)


def _dot_hi(a, b):
    return jnp.dot(a, b, preferred_element_type=f32, precision=lax.Precision.HIGHEST)


def _dot_3pass(a, b):
    a_hi = a.astype(bf16)
    a_lo = (a - a_hi.astype(f32)).astype(bf16)
    b_hi = b.astype(bf16)
    b_lo = (b - b_hi.astype(f32)).astype(bf16)
    return _dot(a_hi, b_hi) + _dot(a_hi, b_lo) + _dot(a_lo, b_hi)


def _dot_nt(a, b):
    return lax.dot_general(a, b, (((1,), (1,)), ((), ())), preferred_element_type=f32)


def _dot_tn(a, b):
    return lax.dot_general(a, b, (((0,), (0,)), ((), ())), preferred_element_type=f32)


def _split3(x):
    hi = x.astype(bf16)
    r1 = x - hi.astype(f32)
    mid = r1.astype(bf16)
    lo = (r1 - mid.astype(f32)).astype(bf16)
    return hi, mid, lo


def _silu(x):
    return x / (1.0 + jnp.exp(-x))


def _softplus(x):
    return jnp.maximum(x, 0.0) + jnp.log(1.0 + jnp.exp(-jnp.abs(x)))


def _const_spec(shape):
    nd = len(shape)
    return pl.BlockSpec(shape, lambda *_: (0,) * nd)


def _inproj_kernel(xp_ref, x_ref, xn_ref, w_ref, wdtt_ref, hcw_ref, hcb_ref,
                   scw_ref, scb_ref, x0_ref, z_ref, gate_ref, xbc_ref, dtt_ref, pbuf):
    tm = x_ref.shape[1]
    i = pl.program_id(1)
    xp = jnp.where(i > 0, _rms(xp_ref[0]), 0.0)
    xn = jnp.where(i < pl.num_programs(1) - 1, _rms(xn_ref[0]), 0.0)
    xm = _rms(x_ref[0])
    xe = jnp.concatenate([xp, xm, xn], axis=0).astype(bf16)
    xmb = xe[HALO:HALO + tm]

    def conv_slabs(buf, col0, cw_ref, cb_ref, taps, wcol0):
        p = _dot(xe, w_ref[:, col0:col0 + HY_WIDTH])
        for s in range(HY_SLABS):
            pbuf[buf, s] = p[:, s * LANES:(s + 1) * LANES]
        pad = taps // 2
        outs = []
        for s in range(HY_SLABS):
            cs = slice(wcol0 + s * LANES, wcol0 + (s + 1) * LANES)
            acc = cb_ref[:, cs]
            for k in range(taps):
                acc = acc + pbuf[buf, s, pl.ds(HALO + k - pad, tm), :] * cw_ref[k:k + 1, cs]
            outs.append(acc)
        return outs

    for s, blk in enumerate(conv_slabs(0, 0, hcw_ref, hcb_ref, HY_CONV, 0)):
        x0_ref[0, :, s * LANES:(s + 1) * LANES] = blk.astype(x0_ref.dtype)
    for s, blk in enumerate(conv_slabs(1, HY_WIDTH, hcw_ref, hcb_ref, HY_CONV, HY_WIDTH)):
        z_ref[0, s] = blk
    for s, blk in enumerate(conv_slabs(0, 2 * HY_WIDTH, hcw_ref, hcb_ref, HY_CONV, 2 * HY_WIDTH)):
        z_ref[0, s] = z_ref[0, s] * blk
    gate_ref[0] = _dot(xmb, w_ref[:, HY_IN:HY_IN + SSM_WIDTH]).astype(gate_ref.dtype)
    o2 = HY_IN + SSM_WIDTH
    for c in range(XBC_WIDTH // HY_WIDTH):
        for s, blk in enumerate(conv_slabs(1 - c % 2, o2 + c * HY_WIDTH, scw_ref, scb_ref, SSM_CONV, c * HY_WIDTH)):
            xbc_ref[0, :, c * HY_WIDTH + s * LANES:c * HY_WIDTH + (s + 1) * LANES] = _silu(blk).astype(xbc_ref.dtype)
    dtt_ref[0] = _dot_nt(wdtt_ref[...], xmb)


def _inproj(x, w_main, w_dtt, hcw, hcb, scw, scb):
    b, l, d = x.shape
    tm = ROW_TILE
    nt = l // tm
    hb = tm // HALO
    grid = (b, nt)
    out_shape = (
        jax.ShapeDtypeStruct((b, l, HY_WIDTH), bf16),
        jax.ShapeDtypeStruct((b, HY_SLABS, l, LANES), f32),
        jax.ShapeDtypeStruct((b, l, SSM_WIDTH), bf16),
        jax.ShapeDtypeStruct((b, l, XBC_WIDTH), bf16),
        jax.ShapeDtypeStruct((b, DT_WIDTH, l), f32),
    )
    in_specs = [
        pl.BlockSpec((1, HALO, d), lambda bi, i: (bi, jnp.maximum(i * hb - 1, 0), 0)),
        pl.BlockSpec((1, tm, d), lambda bi, i: (bi, i, 0)),
        pl.BlockSpec((1, HALO, d), lambda bi, i: (bi, jnp.minimum((i + 1) * hb, l // HALO - 1), 0)),
        _const_spec(w_main.shape),
        _const_spec(w_dtt.shape), _const_spec(hcw.shape), _const_spec(hcb.shape),
        _const_spec(scw.shape), _const_spec(scb.shape),
    ]
    out_specs = (
        pl.BlockSpec((1, tm, HY_WIDTH), lambda bi, i: (bi, i, 0)),
        pl.BlockSpec((1, HY_SLABS, tm, LANES), lambda bi, i: (bi, 0, i, 0)),
        pl.BlockSpec((1, tm, SSM_WIDTH), lambda bi, i: (bi, i, 0)),
        pl.BlockSpec((1, tm, XBC_WIDTH), lambda bi, i: (bi, i, 0)),
        pl.BlockSpec((1, DT_WIDTH, tm), lambda bi, i: (bi, 0, i)),
    )
    return pl.pallas_call(
        _inproj_kernel, grid=grid, in_specs=in_specs, out_specs=out_specs, out_shape=out_shape,
        scratch_shapes=[pltpu.VMEM((2, HY_SLABS, tm + 2 * HALO, LANES), f32)],
        compiler_params=_cparams("parallel", "parallel"), name="inproj",
    )(x, x, x, w_main, w_dtt, hcw, hcb, scw, scb)


def _spectra_kernel(fw1t_ref, fw1c_ref, fw1s_ref, fb1_ref, fw2_ref, fb2_ref, fw3_ref, fb3_ref,
                    freq_ref, tc_ref, ts_ref, kre_ref, kim_ref, *, seq_len, phases):
    d = pl.program_id(0)
    lane = lax.broadcasted_iota(jnp.int32, (1, 2 * SUB_LEN), 1)
    m = lane % SUB_LEN
    pos = jnp.abs(jnp.where(lane >= SUB_LEN, d - phases * m, d + phases * m)).astype(f32)
    t = pos * (1.0 / (seq_len - 1))
    w = pos * (2.0 * math.pi / seq_len)
    band = lax.broadcasted_iota(jnp.int32, (HY_BANDS, 1), 0).astype(f32)
    fband = 1e-4 + band * ((HY_BANDS - 1 - 1e-4) / (HY_BANDS - 1))
    ang = fband * w
    fr = freq_ref[...]
    pre = fw1t_ref[...] * t + _dot_hi(fw1c_ref[...], jnp.cos(ang)) - _dot_hi(fw1s_ref[...], jnp.sin(ang))
    h = jnp.sin(fr * (pre + fb1_ref[...]))
    h = jnp.sin(fr * (_dot_hi(fw2_ref[...], h) + fb2_ref[...]))
    o = _dot_3pass(fw3_ref[...], h) + fb3_ref[...]
    max_decay = math.log(HY_DECAY_TARGET) / HY_FAST_PCT
    min_decay = math.log(HY_DECAY_TARGET) / HY_SLOW_PCT
    ch = lax.broadcasted_iota(jnp.int32, (HY_WIDTH, 1), 0).astype(f32)
    deltas = jnp.abs(min_decay + ch * ((max_decay - min_decay) / (HY_WIDTH - 1)))
    decay = jnp.exp(-deltas * t)
    fwd = o[:HY_WIDTH] * decay
    bwd = o[HY_WIDTH:] * decay
    lag0 = m[:, :SUB_LEN] == 0
    gp = fwd[:, :SUB_LEN]
    gm = jnp.where(lag0, 0.0, bwd[:, SUB_LEN:])
    kre_ref[0, 0] = _dot_nt(tc_ref[...], (gp + gm).astype(bf16))
    kim_ref[0, 0] = _dot_nt(ts_ref[...], (gp - gm).astype(bf16))
    gp = jnp.where(jnp.logical_and(lag0, d > 0), bwd[:, SUB_LEN:], fwd[:, SUB_LEN:])
    gm = jnp.where(lag0, 0.0, bwd[:, :SUB_LEN])
    kre_ref[1, 0] = _dot_nt(tc_ref[...], (gp + gm).astype(bf16))
    kim_ref[1, 0] = _dot_nt(ts_ref[...], (gp - gm).astype(bf16))


def _filter_spectra(seq_len, fw1, fb1, fw2, fb2, fw3, fb3, freq, tc, ts):
    phases = seq_len // SUB_LEN
    col = lambda a: a.reshape(-1, 1)
    args = (col(fw1[0]), fw1[1:1 + HY_BANDS].T, fw1[1 + HY_BANDS:].T, col(fb1), fw2.T, col(fb2), fw3.T, col(fb3),
            col(freq), tc, ts)
    out_shape = (jax.ShapeDtypeStruct((2, phases, SUB_LEN, HY_WIDTH), f32),) * 2
    spec = pl.BlockSpec((2, 1, SUB_LEN, HY_WIDTH), lambda k: (0, k, 0, 0))
    return pl.pallas_call(
        functools.partial(_spectra_kernel, seq_len=seq_len, phases=phases),
        grid=(phases,), in_specs=[_const_spec(a.shape) for a in args], out_specs=(spec, spec),
        out_shape=out_shape, compiler_params=_cparams("parallel"), name="filter_spectra",
    )(*args)


def _dft_tables():
    f = jnp.arange(SUB_LEN, dtype=jnp.int32)[:, None]
    n = jnp.arange(SUB_LEN, dtype=jnp.int32)[None, :]
    ang = (((2 * f + 1) * n) % (2 * SUB_FFT)).astype(f32) * (math.pi / SUB_FFT)
    c, s = jnp.cos(ang), jnp.sin(ang)
    tc, ts = c.astype(bf16), (-s).astype(bf16)
    scale = 2.0 / SUB_FFT
    ic = (scale * c).T.astype(bf16)
    isn = (-scale * s).T.astype(bf16)
    return tc, ts, ic, isn


def _longconv_kernel(z_ref, kre_ref, kim_ref, tc_ref, ts_ref, ic_ref, is_ref, o_ref,
                     zre, zim, wre, wim, *, phases):
    slabs = z_ref.shape[1]
    for q in range(phases):
        zq = jnp.concatenate(
            [z_ref[0, s, pl.ds(q, SUB_LEN, stride=phases), :] for s in range(slabs)], axis=1).astype(bf16)
        zre[q] = _dot(tc_ref[...], zq)
        zim[q] = _dot(ts_ref[...], zq)

    def pointwise(p, slot):
        for r in range(SUB_LEN // PW_ROWS):
            rs = slice(r * PW_ROWS, (r + 1) * PW_ROWS)
            are = jnp.zeros((PW_ROWS, zre.shape[2]), f32)
            aim = jnp.zeros((PW_ROWS, zre.shape[2]), f32)
            for q in range(phases):
                kd = jnp.where(p >= q, p - q, phases + q - p)
                kr, ki = kre_ref[kd, rs, :], kim_ref[kd, rs, :]
                xr, xi = zre[q, rs, :], zim[q, rs, :]
                are = are + (kr * xr - ki * xi)
                aim = aim + (kr * xi + ki * xr)
            wre[slot, rs, :] = are.astype(bf16)
            wim[slot, rs, :] = aim.astype(bf16)

    def inverse(p, slot):
        yp = _dot(ic_ref[...], wre[slot]) + _dot(is_ref[...], wim[slot])
        for s in range(slabs):
            o_ref[0, s, pl.ds(p, SUB_LEN, stride=phases), :] = yp[:, s * LANES:(s + 1) * LANES]

    pointwise(0, 0)

    def body(p, carry):
        pointwise(p, p % 2)
        inverse(p - 1, (p - 1) % 2)
        return carry

    lax.fori_loop(1, phases, body, 0)
    inverse(phases - 1, (phases - 1) % 2)


def _longconv(z4, kre, kim, tcf, tsf, ic, isn):
    b, slabs, l, _ = z4.shape
    phases = l // SUB_LEN
    sb = 2
    cb = sb * LANES
    zspec = pl.BlockSpec((1, sb, l, LANES), lambda j, bi: (bi, j, 0, 0))
    spectra_bytes = 2 * (2 * phases) * SUB_LEN * cb * 4
    kspec = pl.BlockSpec((2 * phases, SUB_LEN, cb), lambda j, bi: (0, 0, j),
                         pipeline_mode=pl.Buffered(1 if 2 * spectra_bytes > VMEM_LIMIT // 2 else 2))
    tspec = _const_spec(tcf.shape)
    return pl.pallas_call(
        functools.partial(_longconv_kernel, phases=phases),
        grid=(slabs // sb, b),
        in_specs=[zspec, kspec, kspec, tspec, tspec, tspec, tspec], out_specs=zspec,
        out_shape=jax.ShapeDtypeStruct(z4.shape, f32),
        scratch_shapes=[pltpu.VMEM((phases, SUB_LEN, cb), f32), pltpu.VMEM((phases, SUB_LEN, cb), f32),
                        pltpu.VMEM((2, SUB_LEN, cb), bf16), pltpu.VMEM((2, SUB_LEN, cb), bf16)],
        compiler_params=_cparams("parallel", "parallel"), name="longconv",
    )(z4, kre.reshape(2 * phases, SUB_LEN, HY_WIDTH), kim.reshape(2 * phases, SUB_LEN, HY_WIDTH),
      tcf, tsf, ic, isn)


SSD_ROWS_PER_DIR = 5 * SSM_HEADS
SSD_UNROLL = 8


def _ssd_bidir_kernel(xbc_ref, dtt_ref, dtb_ref, al_ref, ex_ref, gate_ref, dskip_ref, y_ref,
                      smat, rtmat, yacc, stash, hf_ref, hb_ref):
    q = SSM_CHUNK
    nh = SSM_HEADS
    n = xbc_ref.shape[1] // q
    half = n // 2
    rows_dir = SSD_ROWS_PER_DIR
    hp = (nh // SSM_GROUPS) * SSM_HEAD_DIM
    log2e = 1.0 / math.log(2.0)

    ii = lax.broadcasted_iota(jnp.int32, (q, q), 0)
    jj = lax.broadcasted_iota(jnp.int32, (q, q), 1)
    lower, upper = ii >= jj, ii <= jj
    first_head = lax.broadcasted_iota(jnp.int32, (q, LANES), 1) < SSM_HEAD_DIM

    dt16 = _softplus(dtt_ref[0] + dtb_ref[...])
    da16 = dt16 * (-jnp.exp(al_ref[...]))
    stack = lambda a: jnp.concatenate([a[:, c * q:(c + 1) * q] for c in range(n)], axis=0)
    smat[:, 2 * rows_dir:, :] = jnp.zeros((n, q - 2 * rows_dir, q), f32)
    for d in range(2):
        m_sum = jnp.where(upper if d == 0 else lower, 1.0, 0.0).astype(bf16)
        dt = stack(dt16[d * nh:(d + 1) * nh])
        cs2 = sum(_dot(part, m_sum) for part in _split3(stack(da16[d * nh:(d + 1) * nh]))) * log2e
        edge = cs2[:, q - 1:q] if d == 0 else cs2[:, 0:1]
        e = jnp.exp2(cs2)
        dd = dt * jnp.exp2(edge - cs2)
        e_hi = e.astype(bf16).astype(f32)
        dd_hi = dd.astype(bf16).astype(f32)
        for j, a in enumerate((cs2, e_hi, e - e_hi, dd_hi, dd - dd_hi)):
            smat[:, d * rows_dir + j * nh:d * rows_dir + (j + 1) * nh, :] = a.reshape(n, nh, q)
        rtmat[:, d * nh:(d + 1) * nh, :] = (cs2 - jnp.log2(dt)).reshape(n, nh, q)
    hf_ref[...] = jnp.zeros_like(hf_ref)
    hb_ref[...] = jnp.zeros_like(hb_ref)

    def chunk_rows(c):
        return pl.ds(pl.multiple_of(c * q, q), q)

    def in_chunk(c, d, h_ref):
        rows = chunk_rows(c)
        xsb = xbc_ref[0, rows, 0:SSM_WIDTH]
        bm = xbc_ref[0, rows, SSM_WIDTH:SSM_WIDTH + BC_WIDTH]
        cm = xbc_ref[0, rows, SSM_WIDTH + BC_WIDTH:XBC_WIDTH]
        xs = xsb.astype(f32)
        cols = smat[c].T
        wide = _dot(cols.astype(bf16), ex_ref[...])
        e_f, dd_f = wide[:, 0:SSM_WIDTH], wide[:, SSM_WIDTH:2 * SSM_WIDTH]
        e_b, dd_b = wide[:, 2 * SSM_WIDTH:3 * SSM_WIDTH], wide[:, 3 * SSM_WIDTH:]
        xdd_f = (xs * dd_f).astype(bf16)
        xdd_b = (xs * dd_b).astype(bf16)
        rt = rtmat[c]
        cstack = jnp.concatenate([cm[:, :SSM_STATE], cm[:, SSM_STATE:]], axis=0)
        bstack = jnp.concatenate([bm[:, :SSM_STATE], bm[:, SSM_STATE:]], axis=0)
        cb_all = _dot_nt(cstack, bstack)
        ys, sts_f, sts_b = [], [], []
        for g in range(SSM_GROUPS):
            bg = bm[:, g * SSM_STATE:(g + 1) * SSM_STATE]
            gl = slice(g * hp, (g + 1) * hp)
            cb = cb_all[g * q:(g + 1) * q, g * q:(g + 1) * q]
            st = _dot_tn(bg, jnp.concatenate([xdd_f[:, gl], xdd_b[:, gl]], axis=1))
            sts_f.append(st[:, :hp])
            sts_b.append(st[:, hp:])
            for pr in range(nh // SSM_GROUPS // 2):
                ms = []
                for hd in (g * (nh // SSM_GROUPS) + 2 * pr, g * (nh // SSM_GROUPS) + 2 * pr + 1):
                    df = cols[:, hd:hd + 1] - rt[hd:hd + 1, :]
                    db = cols[:, rows_dir + hd:rows_dir + hd + 1] - rt[nh + hd:nh + hd + 1, :]
                    lm = jnp.exp2(jnp.where(lower, df, -1e30)) + jnp.exp2(jnp.where(upper, db, -1e30))
                    ms.append((cb * lm).astype(bf16))
                xp = xsb[:, g * hp + pr * LANES:g * hp + (pr + 1) * LANES]
                zero = jnp.zeros_like(xp)
                rhs = jnp.concatenate([jnp.where(first_head, xp, zero), jnp.where(first_head, zero, xp)], axis=0)
                ys.append(_dot(jnp.concatenate(ms, axis=1), rhs))
        st = (jnp.concatenate(sts_f, axis=1), jnp.concatenate(sts_b, axis=1))
        yoff = carried(c, h_ref, (e_f, e_b)[d], st[d], (q - 1, 0)[d])
        yacc[rows, :] = jnp.concatenate(ys, axis=1) + xs * dskip_ref[...] + yoff
        return st[1 - d]

    def expand_e(c, d):
        cols = smat[c].T
        return _dot(cols.astype(bf16), ex_ref[:, 2 * d * SSM_WIDTH:(2 * d + 1) * SSM_WIDTH])

    def carried(c, h_ref, e_x, st, edge_row):
        cm = xbc_ref[0, chunk_rows(c), SSM_WIDTH + BC_WIDTH:XBC_WIDTH]
        h = h_ref[...]
        hb = h.astype(bf16)
        zh = jnp.zeros((SSM_STATE, hp), bf16)
        h_blocks = jnp.concatenate([jnp.concatenate([hb[:, :hp], zh], axis=1),
                                    jnp.concatenate([zh, hb[:, hp:]], axis=1)], axis=0)
        h_ref[...] = h * e_x[edge_row:edge_row + 1, :] + st
        return _dot(cm, h_blocks) * e_x

    def finish(c, y):
        rows = chunk_rows(c)
        gated = y * _silu(gate_ref[0, rows, :].astype(f32))
        y_ref[0, rows, :] = _rms(gated).astype(y_ref.dtype)

    def first_half(k, carry):
        m = n - 1 - k
        stash[k] = in_chunk(k, 0, hf_ref)
        stash[m] = in_chunk(m, 1, hb_ref)
        return carry

    def second_half(k, carry):
        m = n - 1 - k
        finish(k, yacc[chunk_rows(k), :] + carried(k, hf_ref, expand_e(k, 0), stash[k], q - 1))
        finish(m, yacc[chunk_rows(m), :] + carried(m, hb_ref, expand_e(m, 1), stash[m], 0))
        return carry

    lax.fori_loop(0, half, first_half, 0, unroll=SSD_UNROLL)
    lax.fori_loop(half, n, second_half, 0, unroll=SSD_UNROLL)


def _ssd_bidir(xbc, dtt, gate, dtb, alog, expand, dskip):
    b, l, _ = xbc.shape
    n = l // SSM_CHUNK
    seq = lambda w: pl.BlockSpec((1, l, w), lambda bi: (bi, 0, 0))
    small = (dtb.reshape(DT_WIDTH, 1), alog.reshape(DT_WIDTH, 1), expand)
    return pl.pallas_call(
        _ssd_bidir_kernel, grid=(b,),
        in_specs=[seq(XBC_WIDTH), pl.BlockSpec((1, DT_WIDTH, l), lambda bi: (bi, 0, 0)),
                  *[_const_spec(a.shape) for a in small], seq(SSM_WIDTH),
                  _const_spec(dskip.shape)],
        out_specs=seq(SSM_WIDTH), out_shape=jax.ShapeDtypeStruct((b, l, SSM_WIDTH), bf16),
        scratch_shapes=[pltpu.VMEM((n, SSM_CHUNK, SSM_CHUNK), f32),
                        pltpu.VMEM((n, DT_WIDTH, SSM_CHUNK), f32),
                        pltpu.VMEM((l, SSM_WIDTH), f32),
                        pltpu.VMEM((n, SSM_STATE, SSM_WIDTH), f32),
                        pltpu.VMEM((SSM_STATE, SSM_WIDTH), f32), pltpu.VMEM((SSM_STATE, SSM_WIDTH), f32)],
        compiler_params=_cparams("parallel"), name="ssd_bidir",
    )(xbc, dtt, *small, gate, dskip)


def _ssd_bidir_expand_matrix():
    row = jnp.arange(SSM_CHUNK, dtype=jnp.int32)[:, None]
    lane = jnp.arange(4 * SSM_WIDTH, dtype=jnp.int32)[None, :]
    head = (lane % SSM_WIDTH) // SSM_HEAD_DIM
    quantity = (lane // SSM_WIDTH) % 2
    direction = lane // (2 * SSM_WIDTH)
    base = direction * SSD_ROWS_PER_DIR + SSM_HEADS + 2 * SSM_HEADS * quantity
    hit = jnp.logical_or(row == base + head, row == base + SSM_HEADS + head)
    return hit.astype(bf16)


def _kv_kernel(mem_ref, wkv_ref, k_ref, v_ref):
    mn = _rms(mem_ref[0]).astype(bf16)
    k_ref[0] = _dot(mn, wkv_ref[:, :D_MODEL]).astype(bf16)
    v_ref[0] = _dot(mn, wkv_ref[:, D_MODEL:]).astype(bf16)


def _kv_proj(mem, wkv):
    b, m, d = mem.shape
    spec = pl.BlockSpec((1, m, d), lambda bi: (bi, 0, 0))
    return pl.pallas_call(
        _kv_kernel, grid=(b,), in_specs=[spec, _const_spec(wkv.shape)],
        out_specs=(spec, spec), out_shape=(jax.ShapeDtypeStruct((b, m, d), bf16),) * 2,
        compiler_params=_cparams("parallel"), name="kv_proj",
    )(mem, wkv)


def _mix_attn_kernel(x_ref, x0_ref, z_ref, zc_ref, ys_ref, hb_ref, wout_ref, wq_ref,
                     k_ref, v_ref, wo_ref, o_ref):
    z = jnp.concatenate([z_ref[0, s] for s in range(HY_SLABS)], axis=1)
    zc = jnp.concatenate([zc_ref[0, s] for s in range(HY_SLABS)], axis=1)
    y_hy = _rms(x0_ref[0].astype(f32) * (zc + z * hb_ref[...])).astype(bf16)
    x1 = x_ref[0] + _dot(jnp.concatenate([y_hy, ys_ref[0]], axis=1), wout_ref[...])
    xn = _rms(x1).astype(bf16)
    qa = _dot(xn, wq_ref[...]).astype(bf16)
    heads = []
    for h in range(XA_HEADS):
        hs = slice(h * XA_HEAD_DIM, (h + 1) * XA_HEAD_DIM)
        s = _dot_nt(qa[:, hs], k_ref[0, :, hs])
        e = jnp.exp(s - jnp.max(s, axis=-1, keepdims=True))
        den = jnp.sum(e, axis=-1, keepdims=True)
        heads.append(_dot(e.astype(bf16), v_ref[0, :, hs]) / den)
    o = jnp.concatenate(heads, axis=1).astype(bf16)
    o_ref[0] = x1 + _dot(o, wo_ref[...])


def _mix_attn(x, x0, z4, zc4, ys, hbias, wout, wq, k, v, wo):
    b, l, d = x.shape
    tm = ROW_TILE
    row = lambda w: pl.BlockSpec((1, tm, w), lambda bi, i: (bi, i, 0))
    slab = pl.BlockSpec((1, HY_SLABS, tm, LANES), lambda bi, i: (bi, 0, i, 0))
    memspec = pl.BlockSpec((1, N_MEM, d), lambda bi, i: (bi, 0, 0))
    consts = (hbias, wout, wq)
    return pl.pallas_call(
        _mix_attn_kernel, grid=(b, l // tm),
        in_specs=[row(d), row(HY_WIDTH), slab, slab, row(SSM_WIDTH), *[_const_spec(a.shape) for a in consts],
                  memspec, memspec, _const_spec(wo.shape)],
        out_specs=row(d), out_shape=jax.ShapeDtypeStruct((b, l, d), f32),
        compiler_params=_cparams("parallel", "parallel"), name="mix_attn",
    )(x, x0, z4, zc4, ys, *consts, k, v, wo)


def _mlp_kernel(x_ref, wup_ref, wdn_ref, fw_ref, o_ref, *, final):
    x = x_ref[0]
    xn = _rms(x).astype(bf16)
    acc = x
    for c in range(D_FF // D_MODEL):
        cs = slice(c * D_MODEL, (c + 1) * D_MODEL)
        hdn = jnp.maximum(_dot(xn, wup_ref[:, cs]), 0.0)
        acc = acc + _dot((hdn * hdn).astype(bf16), wdn_ref[cs, :])
    o_ref[0] = _rms(acc, fw_ref[...]) if final else acc


def _mlp(x, wup, wdn, fw, final):
    b, l, d = x.shape
    tm = ROW_TILE
    row = pl.BlockSpec((1, tm, d), lambda bi, i: (bi, i, 0))
    wspec = lambda a: pl.BlockSpec(a.shape, lambda bi, i: (0, 0), pipeline_mode=pl.Buffered(1))
    return pl.pallas_call(
        functools.partial(_mlp_kernel, final=final), grid=(b, l // tm),
        in_specs=[row, wspec(wup), wspec(wdn), _const_spec(fw.shape)],
        out_specs=row, out_shape=jax.ShapeDtypeStruct((b, l, d), f32),
        compiler_params=_cparams("parallel", "parallel"), name="mlp",
    )(x, wup, wdn, fw)


def _trunk(x, mem, p, tables):
    tc, ts, ic, isn = tables
    seq_len = x.shape[1]
    for i in range(DEPTH):
        lp = p["layers"][i]
        x0, z4, gate, xbc, dtt = _inproj(x, lp["w_main"], lp["w_dtt"],
                                         lp["hy_conv_w"], lp["hy_conv_b"], lp["ssm_conv_w"], lp["ssm_conv_b"])
        kre, kim = _filter_spectra(seq_len, lp["hy_fw1"], lp["hy_fb1"], lp["hy_fw2"], lp["hy_fb2"],
                                   lp["hy_fw3"], lp["hy_fb3"], lp["hy_sin_freq"], tc, ts)
        zc4 = _longconv(z4, kre, kim, tc, ts, ic, isn)
        ys = _ssd_bidir(xbc, dtt, gate, lp["ssm_dt_bias"], lp["ssm_A_log"], p["ssd_expand"], lp["ssm_D_wide"])
        k, v = _kv_proj(mem, lp["w_kv"])
        x = _mix_attn(x, x0, z4, zc4, ys, lp["hy_bias"], lp["w_out"], lp["w_q"], k, v, lp["w_o"])
        x = _mlp(x, lp["w_up"], lp["w_down"], p["norm_final"], final=(i == DEPTH - 1))
    return x


def kernel(x_prompt, x_sample, mem_prompt, mem_sample, norm_mix, w_in, hy_conv_w, hy_conv_b, hy_fw1, hy_fb1, hy_fw2, hy_fb2, hy_fw3, hy_fb3, hy_sin_freq, hy_bias, hy_norm, ssm_conv_w, ssm_conv_b, ssm_dt_bias, ssm_A_log, ssm_D, ssm_norm, w_out, norm_xattn, norm_mem, w_q, w_kv, w_o, norm_mlp, w_up, w_down, norm_final):
    row = lambda a: a.reshape(1, -1)
    gained = lambda g, w: (g[:, None] * w).astype(bf16)
    layers = []
    for i in range(DEPTH):
        wi = gained(norm_mix[i], w_in[i])
        layers.append(dict(
            w_main=wi, w_dtt=wi[:, MAIN_IN:].T,
            hy_conv_w=hy_conv_w[i], hy_conv_b=row(hy_conv_b[i]),
            ssm_conv_w=ssm_conv_w[i], ssm_conv_b=row(ssm_conv_b[i]),
            hy_fw1=hy_fw1[i], hy_fb1=hy_fb1[i], hy_fw2=hy_fw2[i], hy_fb2=hy_fb2[i], hy_fw3=hy_fw3[i],
            hy_fb3=hy_fb3[i], hy_sin_freq=hy_sin_freq[i], hy_bias=row(hy_bias[i]),
            ssm_dt_bias=ssm_dt_bias[i], ssm_A_log=ssm_A_log[i],
            ssm_D_wide=row(jnp.repeat(ssm_D[i], SSM_HEAD_DIM)),
            w_out=gained(jnp.concatenate([hy_norm[i], ssm_norm[i]]), w_out[i]),
            w_q=gained(norm_xattn[i] * (XA_HEAD_DIM ** -0.5), w_q[i]),
            w_kv=gained(norm_mem[i], w_kv[i]), w_o=w_o[i].astype(bf16),
            w_up=gained(norm_mlp[i], w_up[i]), w_down=w_down[i].astype(bf16),
        ))
    p = dict(layers=layers, norm_final=row(norm_final), ssd_expand=_ssd_bidir_expand_matrix())
    tables = _dft_tables()
    return (_trunk(x_prompt, mem_prompt, p, tables), _trunk(x_sample, mem_sample, p, tables))
```

```python
import functools
import math

import jax
import jax.numpy as jnp
from jax import lax
from jax.experimental import pallas as pl
from jax.experimental.pallas import tpu as pltpu

f32 = jnp.float32
bf16 = jnp.bfloat16

D_MODEL = 1024
DEPTH = 2
N_MEM = 256
HY_WIDTH = 512
HY_CONV = 3
HY_BANDS = 16
HY_DECAY_TARGET = 1e-2
HY_FAST_PCT = 0.3
HY_SLOW_PCT = 1.5
SSM_WIDTH = 512
SSM_HEAD_DIM = 64
SSM_HEADS = 8
SSM_GROUPS = 2
SSM_STATE = 128
SSM_CONV = 5
SSM_CHUNK = 128
BC_WIDTH = SSM_GROUPS * SSM_STATE
XBC_WIDTH = SSM_WIDTH + 2 * BC_WIDTH
HY_IN = 3 * HY_WIDTH
MAIN_IN = HY_IN + SSM_WIDTH + XBC_WIDTH
DT_WIDTH = 2 * SSM_HEADS
XA_HEADS = 4
XA_HEAD_DIM = D_MODEL // XA_HEADS
D_FF = 4 * D_MODEL
EPS = 1e-5

LANES = 128
SUBLANES = 8
HALO = SUBLANES
SUB_LEN = 512
SUB_FFT = 2 * SUB_LEN
HY_SLABS = HY_WIDTH // LANES
ROW_TILE = 1024
PW_ROWS = 32
VMEM_LIMIT = 56 * 1024 * 1024


def _cparams(*sem):
    return pltpu.CompilerParams(dimension_semantics=sem, vmem_limit_bytes=VMEM_LIMIT)


def _rms(x, w=None):
    y = x * lax.rsqrt(jnp.mean(x * x, axis=-1, keepdims=True) + EPS)
    return y if w is None else y * w


def _dot(a, b):
    return jnp.dot(a, b, preferred_element_type=f32)


def _dot_hi(a, b):
    return jnp.dot(a, b, preferred_element_type=f32, precision=lax.Precision.HIGHEST)


def _dot_3pass(a, b):
    a_hi = a.astype(bf16)
    a_lo = (a - a_hi.astype(f32)).astype(bf16)
    b_hi = b.astype(bf16)
    b_lo = (b - b_hi.astype(f32)).astype(bf16)
    return _dot(a_hi, b_hi) + _dot(a_hi, b_lo) + _dot(a_lo, b_hi)


def _dot_nt(a, b):
    return lax.dot_general(a, b, (((1,), (1,)), ((), ())), preferred_element_type=f32)


def _dot_tn(a, b):
    return lax.dot_general(a, b, (((0,), (0,)), ((), ())), preferred_element_type=f32)


def _split3(x):
    hi = x.astype(bf16)
    r1 = x - hi.astype(f32)
    mid = r1.astype(bf16)
    lo = (r1 - mid.astype(f32)).astype(bf16)
    return hi, mid, lo


def _silu(x):
    return x / (1.0 + jnp.exp(-x))


def _softplus(x):
    return jnp.maximum(x, 0.0) + jnp.log(1.0 + jnp.exp(-jnp.abs(x)))


def _const_spec(shape):
    nd = len(shape)
    return pl.BlockSpec(shape, lambda *_: (0,) * nd)


def _layer_spec(stacked, layer, pipeline_mode=None):
    nd = stacked.ndim
    kwargs = {} if pipeline_mode is None else dict(pipeline_mode=pipeline_mode)
    return pl.BlockSpec((None,) + stacked.shape[1:], lambda *_: (layer,) + (0,) * (nd - 1), **kwargs)


def _inproj_kernel(xp_ref, x_ref, xn_ref, w_ref, wdtt_ref, hcw_ref, hcb_ref,
                   scw_ref, scb_ref, x0_ref, z_ref, gate_ref, xbc_ref, dtt_ref, pbuf):
    tm = x_ref.shape[1]
    i = pl.program_id(1)
    xp = jnp.where(i > 0, _rms(xp_ref[0]), 0.0)
    xn = jnp.where(i < pl.num_programs(1) - 1, _rms(xn_ref[0]), 0.0)
    xm = _rms(x_ref[0])
    xe = jnp.concatenate([xp, xm, xn], axis=0).astype(bf16)
    xmb = xe[HALO:HALO + tm]

    def conv_slabs(buf, col0, cw_ref, cb_ref, taps, wcol0):
        p = _dot(xe, w_ref[:, col0:col0 + HY_WIDTH])
        for s in range(HY_SLABS):
            pbuf[buf, s] = p[:, s * LANES:(s + 1) * LANES]
        pad = taps // 2
        outs = []
        for s in range(HY_SLABS):
            cs = slice(wcol0 + s * LANES, wcol0 + (s + 1) * LANES)
            acc = cb_ref[:, cs]
            for k in range(taps):
                acc = acc + pbuf[buf, s, pl.ds(HALO + k - pad, tm), :] * cw_ref[k:k + 1, cs]
            outs.append(acc)
        return outs

    for s, blk in enumerate(conv_slabs(0, 0, hcw_ref, hcb_ref, HY_CONV, 0)):
        x0_ref[0, :, s * LANES:(s + 1) * LANES] = blk.astype(x0_ref.dtype)
    for s, blk in enumerate(conv_slabs(1, HY_WIDTH, hcw_ref, hcb_ref, HY_CONV, HY_WIDTH)):
        z_ref[0, s] = blk
    for s, blk in enumerate(conv_slabs(0, 2 * HY_WIDTH, hcw_ref, hcb_ref, HY_CONV, 2 * HY_WIDTH)):
        z_ref[0, s] = z_ref[0, s] * blk
    gate_ref[0] = _dot(xmb, w_ref[:, HY_IN:HY_IN + SSM_WIDTH]).astype(gate_ref.dtype)
    o2 = HY_IN + SSM_WIDTH
    for c in range(XBC_WIDTH // HY_WIDTH):
        for s, blk in enumerate(conv_slabs(1 - c % 2, o2 + c * HY_WIDTH, scw_ref, scb_ref, SSM_CONV, c * HY_WIDTH)):
            xbc_ref[0, :, c * HY_WIDTH + s * LANES:c * HY_WIDTH + (s + 1) * LANES] = _silu(blk).astype(xbc_ref.dtype)
    dtt_ref[0] = _dot_nt(wdtt_ref[...], xmb)


def _inproj(x, w_main, w_dtt, hcw, hcb, scw, scb, layer):
    b, l, d = x.shape
    tm = ROW_TILE
    nt = l // tm
    hb = tm // HALO
    grid = (b, nt)
    out_shape = (
        jax.ShapeDtypeStruct((b, l, HY_WIDTH), bf16),
        jax.ShapeDtypeStruct((b, HY_SLABS, l, LANES), f32),
        jax.ShapeDtypeStruct((b, l, SSM_WIDTH), bf16),
        jax.ShapeDtypeStruct((b, l, XBC_WIDTH), bf16),
        jax.ShapeDtypeStruct((b, DT_WIDTH, l), f32),
    )
    in_specs = [
        pl.BlockSpec((1, HALO, d), lambda bi, i: (bi, jnp.maximum(i * hb - 1, 0), 0)),
        pl.BlockSpec((1, tm, d), lambda bi, i: (bi, i, 0)),
        pl.BlockSpec((1, HALO, d), lambda bi, i: (bi, jnp.minimum((i + 1) * hb, l // HALO - 1), 0)),
        _layer_spec(w_main, layer),
        _layer_spec(w_dtt, layer), _layer_spec(hcw, layer), _layer_spec(hcb, layer),
        _layer_spec(scw, layer), _layer_spec(scb, layer),
    ]
    out_specs = (
        pl.BlockSpec((1, tm, HY_WIDTH), lambda bi, i: (bi, i, 0)),
        pl.BlockSpec((1, HY_SLABS, tm, LANES), lambda bi, i: (bi, 0, i, 0)),
        pl.BlockSpec((1, tm, SSM_WIDTH), lambda bi, i: (bi, i, 0)),
        pl.BlockSpec((1, tm, XBC_WIDTH), lambda bi, i: (bi, i, 0)),
        pl.BlockSpec((1, DT_WIDTH, tm), lambda bi, i: (bi, 0, i)),
    )
    return pl.pallas_call(
        _inproj_kernel, grid=grid, in_specs=in_specs, out_specs=out_specs, out_shape=out_shape,
        scratch_shapes=[pltpu.VMEM((2, HY_SLABS, tm + 2 * HALO, LANES), f32)],
        compiler_params=_cparams("parallel", "parallel"), name="inproj",
    )(x, x, x, w_main, w_dtt, hcw, hcb, scw, scb)


def _spectra_kernel(fw1t_ref, fw1c_ref, fw1s_ref, fb1_ref, fw2_ref, fb2_ref, fw3_ref, fb3_ref,
                    freq_ref, tc_ref, ts_ref, kre_ref, kim_ref, *, seq_len, phases):
    d = pl.program_id(0)
    lane = lax.broadcasted_iota(jnp.int32, (1, 2 * SUB_LEN), 1)
    m = lane % SUB_LEN
    pos = jnp.abs(jnp.where(lane >= SUB_LEN, d - phases * m, d + phases * m)).astype(f32)
    t = pos * (1.0 / (seq_len - 1))
    w = pos * (2.0 * math.pi / seq_len)
    band = lax.broadcasted_iota(jnp.int32, (HY_BANDS, 1), 0).astype(f32)
    fband = 1e-4 + band * ((HY_BANDS - 1 - 1e-4) / (HY_BANDS - 1))
    ang = fband * w
    fr = freq_ref[...]
    pre = fw1t_ref[...] * t + _dot_hi(fw1c_ref[...], jnp.cos(ang)) - _dot_hi(fw1s_ref[...], jnp.sin(ang))
    h = jnp.sin(fr * (pre + fb1_ref[...]))
    h = jnp.sin(fr * (_dot_hi(fw2_ref[...], h) + fb2_ref[...]))
    o = _dot_3pass(fw3_ref[...], h) + fb3_ref[...]
    max_decay = math.log(HY_DECAY_TARGET) / HY_FAST_PCT
    min_decay = math.log(HY_DECAY_TARGET) / HY_SLOW_PCT
    ch = lax.broadcasted_iota(jnp.int32, (HY_WIDTH, 1), 0).astype(f32)
    deltas = jnp.abs(min_decay + ch * ((max_decay - min_decay) / (HY_WIDTH - 1)))
    decay = jnp.exp(-deltas * t)
    fwd = o[:HY_WIDTH] * decay
    bwd = o[HY_WIDTH:] * decay
    lag0 = m[:, :SUB_LEN] == 0
    gp = fwd[:, :SUB_LEN]
    gm = jnp.where(lag0, 0.0, bwd[:, SUB_LEN:])
    kre_ref[0, 0] = _dot_nt(tc_ref[...], (gp + gm).astype(bf16))
    kim_ref[0, 0] = _dot_nt(ts_ref[...], (gp - gm).astype(bf16))
    gp = jnp.where(jnp.logical_and(lag0, d > 0), bwd[:, SUB_LEN:], fwd[:, SUB_LEN:])
    gm = jnp.where(lag0, 0.0, bwd[:, :SUB_LEN])
    kre_ref[1, 0] = _dot_nt(tc_ref[...], (gp + gm).astype(bf16))
    kim_ref[1, 0] = _dot_nt(ts_ref[...], (gp - gm).astype(bf16))


def _filter_spectra(seq_len, filter_params, layer, tc, ts):
    phases = seq_len // SUB_LEN
    out_shape = (jax.ShapeDtypeStruct((2, phases, SUB_LEN, HY_WIDTH), f32),) * 2
    spec = pl.BlockSpec((2, 1, SUB_LEN, HY_WIDTH), lambda k: (0, k, 0, 0))
    return pl.pallas_call(
        functools.partial(_spectra_kernel, seq_len=seq_len, phases=phases),
        grid=(phases,),
        in_specs=[*[_layer_spec(a, layer) for a in filter_params], _const_spec(tc.shape), _const_spec(ts.shape)],
        out_specs=(spec, spec),
        out_shape=out_shape, compiler_params=_cparams("parallel"), name="filter_spectra",
    )(*filter_params, tc, ts)


def _filter_mlp_params(fw1, fb1, fw2, fb2, fw3, fb3, freq):
    col = lambda a: a[:, :, None]
    t = lambda a: jnp.swapaxes(a, 1, 2)
    return (col(fw1[:, 0]), t(fw1[:, 1:1 + HY_BANDS]), t(fw1[:, 1 + HY_BANDS:]), col(fb1), t(fw2), col(fb2),
            t(fw3), col(fb3), col(freq))


def _dft_tables():
    f = jnp.arange(SUB_LEN, dtype=jnp.int32)[:, None]
    n = jnp.arange(SUB_LEN, dtype=jnp.int32)[None, :]
    ang = (((2 * f + 1) * n) % (2 * SUB_FFT)).astype(f32) * (math.pi / SUB_FFT)
    c, s = jnp.cos(ang), jnp.sin(ang)
    tc, ts = c.astype(bf16), (-s).astype(bf16)
    scale = 2.0 / SUB_FFT
    ic = (scale * c).T.astype(bf16)
    isn = (-scale * s).T.astype(bf16)
    return tc, ts, ic, isn


def _longconv_kernel(z_ref, kre_ref, kim_ref, tc_ref, ts_ref, ic_ref, is_ref, o_ref,
                     zre, zim, wre, wim, *, phases):
    slabs = z_ref.shape[1]
    for q in range(phases):
        zq = jnp.concatenate(
            [z_ref[0, s, pl.ds(q, SUB_LEN, stride=phases), :] for s in range(slabs)], axis=1).astype(bf16)
        zre[q] = _dot(tc_ref[...], zq)
        zim[q] = _dot(ts_ref[...], zq)

    def pointwise(p, slot):
        for r in range(SUB_LEN // PW_ROWS):
            rs = slice(r * PW_ROWS, (r + 1) * PW_ROWS)
            are = jnp.zeros((PW_ROWS, zre.shape[2]), f32)
            aim = jnp.zeros((PW_ROWS, zre.shape[2]), f32)
            for q in range(phases):
                kd = jnp.where(p >= q, p - q, phases + q - p)
                kr, ki = kre_ref[kd, rs, :], kim_ref[kd, rs, :]
                xr, xi = zre[q, rs, :], zim[q, rs, :]
                are = are + (kr * xr - ki * xi)
                aim = aim + (kr * xi + ki * xr)
            wre[slot, rs, :] = are.astype(bf16)
            wim[slot, rs, :] = aim.astype(bf16)

    def inverse(p, slot):
        yp = _dot(ic_ref[...], wre[slot]) + _dot(is_ref[...], wim[slot])
        for s in range(slabs):
            o_ref[0, s, pl.ds(p, SUB_LEN, stride=phases), :] = yp[:, s * LANES:(s + 1) * LANES]

    pointwise(0, 0)

    def body(p, carry):
        pointwise(p, p % 2)
        inverse(p - 1, (p - 1) % 2)
        return carry

    lax.fori_loop(1, phases, body, 0)
    inverse(phases - 1, (phases - 1) % 2)


def _longconv(z4, kre, kim, tcf, tsf, ic, isn):
    b, slabs, l, _ = z4.shape
    phases = l // SUB_LEN
    sb = 2
    cb = sb * LANES
    zspec = pl.BlockSpec((1, sb, l, LANES), lambda j, bi: (bi, j, 0, 0))
    spectra_bytes = 2 * (2 * phases) * SUB_LEN * cb * 4
    kspec = pl.BlockSpec((2 * phases, SUB_LEN, cb), lambda j, bi: (0, 0, j),
                         pipeline_mode=pl.Buffered(1 if 2 * spectra_bytes > VMEM_LIMIT // 2 else 2))
    tspec = _const_spec(tcf.shape)
    return pl.pallas_call(
        functools.partial(_longconv_kernel, phases=phases),
        grid=(slabs // sb, b),
        in_specs=[zspec, kspec, kspec, tspec, tspec, tspec, tspec], out_specs=zspec,
        out_shape=jax.ShapeDtypeStruct(z4.shape, f32),
        scratch_shapes=[pltpu.VMEM((phases, SUB_LEN, cb), f32), pltpu.VMEM((phases, SUB_LEN, cb), f32),
                        pltpu.VMEM((2, SUB_LEN, cb), bf16), pltpu.VMEM((2, SUB_LEN, cb), bf16)],
        compiler_params=_cparams("parallel", "parallel"), name="longconv",
    )(z4, kre.reshape(2 * phases, SUB_LEN, HY_WIDTH), kim.reshape(2 * phases, SUB_LEN, HY_WIDTH),
      tcf, tsf, ic, isn)


SSD_ROWS_PER_DIR = 5 * SSM_HEADS
SSD_UNROLL = 8


def _ssd_bidir_kernel(xbc_ref, dtt_ref, dtb_ref, al_ref, ex_ref, gate_ref, dskip_ref, y_ref,
                      smat, rtmat, yacc, stash, hf_ref, hb_ref):
    q = SSM_CHUNK
    nh = SSM_HEADS
    n = xbc_ref.shape[1] // q
    half = n // 2
    rows_dir = SSD_ROWS_PER_DIR
    hp = (nh // SSM_GROUPS) * SSM_HEAD_DIM
    log2e = 1.0 / math.log(2.0)

    ii = lax.broadcasted_iota(jnp.int32, (q, q), 0)
    jj = lax.broadcasted_iota(jnp.int32, (q, q), 1)
    lower, upper = ii >= jj, ii <= jj
    first_head = lax.broadcasted_iota(jnp.int32, (q, LANES), 1) < SSM_HEAD_DIM

    dt16 = _softplus(dtt_ref[0] + dtb_ref[...])
    da16 = dt16 * (-jnp.exp(al_ref[...]))
    stack = lambda a: jnp.concatenate([a[:, c * q:(c + 1) * q] for c in range(n)], axis=0)
    smat[:, 2 * rows_dir:, :] = jnp.zeros((n, q - 2 * rows_dir, q), f32)
    for d in range(2):
        m_sum = jnp.where(upper if d == 0 else lower, 1.0, 0.0).astype(bf16)
        dt = stack(dt16[d * nh:(d + 1) * nh])
        cs2 = sum(_dot(part, m_sum) for part in _split3(stack(da16[d * nh:(d + 1) * nh]))) * log2e
        edge = cs2[:, q - 1:q] if d == 0 else cs2[:, 0:1]
        e = jnp.exp2(cs2)
        dd = dt * jnp.exp2(edge - cs2)
        e_hi = e.astype(bf16).astype(f32)
        dd_hi = dd.astype(bf16).astype(f32)
        for j, a in enumerate((cs2, e_hi, e - e_hi, dd_hi, dd - dd_hi)):
            smat[:, d * rows_dir + j * nh:d * rows_dir + (j + 1) * nh, :] = a.reshape(n, nh, q)
        rtmat[:, d * nh:(d + 1) * nh, :] = (cs2 - jnp.log2(dt)).reshape(n, nh, q)
    hf_ref[...] = jnp.zeros_like(hf_ref)
    hb_ref[...] = jnp.zeros_like(hb_ref)

    def chunk_rows(c):
        return pl.ds(pl.multiple_of(c * q, q), q)

    def in_chunk(c, d, h_ref):
        rows = chunk_rows(c)
        xsb = xbc_ref[0, rows, 0:SSM_WIDTH]
        bm = xbc_ref[0, rows, SSM_WIDTH:SSM_WIDTH + BC_WIDTH]
        cm = xbc_ref[0, rows, SSM_WIDTH + BC_WIDTH:XBC_WIDTH]
        xs = xsb.astype(f32)
        cols = smat[c].T
        wide = _dot(cols.astype(bf16), ex_ref[...])
        e_f, dd_f = wide[:, 0:SSM_WIDTH], wide[:, SSM_WIDTH:2 * SSM_WIDTH]
        e_b, dd_b = wide[:, 2 * SSM_WIDTH:3 * SSM_WIDTH], wide[:, 3 * SSM_WIDTH:]
        xdd_f = (xs * dd_f).astype(bf16)
        xdd_b = (xs * dd_b).astype(bf16)
        rt = rtmat[c]
        cstack = jnp.concatenate([cm[:, :SSM_STATE], cm[:, SSM_STATE:]], axis=0)
        bstack = jnp.concatenate([bm[:, :SSM_STATE], bm[:, SSM_STATE:]], axis=0)
        cb_all = _dot_nt(cstack, bstack)
        ys, sts_f, sts_b = [], [], []
        for g in range(SSM_GROUPS):
            bg = bm[:, g * SSM_STATE:(g + 1) * SSM_STATE]
            gl = slice(g * hp, (g + 1) * hp)
            cb = cb_all[g * q:(g + 1) * q, g * q:(g + 1) * q]
            st = _dot_tn(bg, jnp.concatenate([xdd_f[:, gl], xdd_b[:, gl]], axis=1))
            sts_f.append(st[:, :hp])
            sts_b.append(st[:, hp:])
            for pr in range(nh // SSM_GROUPS // 2):
                ms = []
                for hd in (g * (nh // SSM_GROUPS) + 2 * pr, g * (nh // SSM_GROUPS) + 2 * pr + 1):
                    df = cols[:, hd:hd + 1] - rt[hd:hd + 1, :]
                    db = cols[:, rows_dir + hd:rows_dir + hd + 1] - rt[nh + hd:nh + hd + 1, :]
                    lm = jnp.exp2(jnp.where(lower, df, -1e30)) + jnp.exp2(jnp.where(upper, db, -1e30))
                    ms.append((cb * lm).astype(bf16))
                xp = xsb[:, g * hp + pr * LANES:g * hp + (pr + 1) * LANES]
                zero = jnp.zeros_like(xp)
                rhs = jnp.concatenate([jnp.where(first_head, xp, zero), jnp.where(first_head, zero, xp)], axis=0)
                ys.append(_dot(jnp.concatenate(ms, axis=1), rhs))
        st = (jnp.concatenate(sts_f, axis=1), jnp.concatenate(sts_b, axis=1))
        yoff = carried(c, h_ref, (e_f, e_b)[d], st[d], (q - 1, 0)[d])
        yacc[rows, :] = jnp.concatenate(ys, axis=1) + xs * dskip_ref[...] + yoff
        return st[1 - d]

    def expand_e(c, d):
        cols = smat[c].T
        return _dot(cols.astype(bf16), ex_ref[:, 2 * d * SSM_WIDTH:(2 * d + 1) * SSM_WIDTH])

    def carried(c, h_ref, e_x, st, edge_row):
        cm = xbc_ref[0, chunk_rows(c), SSM_WIDTH + BC_WIDTH:XBC_WIDTH]
        h = h_ref[...]
        hb = h.astype(bf16)
        zh = jnp.zeros((SSM_STATE, hp), bf16)
        h_blocks = jnp.concatenate([jnp.concatenate([hb[:, :hp], zh], axis=1),
                                    jnp.concatenate([zh, hb[:, hp:]], axis=1)], axis=0)
        h_ref[...] = h * e_x[edge_row:edge_row + 1, :] + st
        return _dot(cm, h_blocks) * e_x

    def finish(c, y):
        rows = chunk_rows(c)
        gated = y * _silu(gate_ref[0, rows, :].astype(f32))
        y_ref[0, rows, :] = _rms(gated).astype(y_ref.dtype)

    def first_half(k, carry):
        m = n - 1 - k
        stash[k] = in_chunk(k, 0, hf_ref)
        stash[m] = in_chunk(m, 1, hb_ref)
        return carry

    def second_half(k, carry):
        m = n - 1 - k
        finish(k, yacc[chunk_rows(k), :] + carried(k, hf_ref, expand_e(k, 0), stash[k], q - 1))
        finish(m, yacc[chunk_rows(m), :] + carried(m, hb_ref, expand_e(m, 1), stash[m], 0))
        return carry

    lax.fori_loop(0, half, first_half, 0, unroll=SSD_UNROLL)
    lax.fori_loop(half, n, second_half, 0, unroll=SSD_UNROLL)


def _ssd_bidir(xbc, dtt, gate, dtb, alog, expand, dskip, layer):
    b, l, _ = xbc.shape
    n = l // SSM_CHUNK
    seq = lambda w: pl.BlockSpec((1, l, w), lambda bi: (bi, 0, 0))
    return pl.pallas_call(
        _ssd_bidir_kernel, grid=(b,),
        in_specs=[seq(XBC_WIDTH), pl.BlockSpec((1, DT_WIDTH, l), lambda bi: (bi, 0, 0)),
                  _layer_spec(dtb, layer), _layer_spec(alog, layer), _const_spec(expand.shape), seq(SSM_WIDTH),
                  _layer_spec(dskip, layer)],
        out_specs=seq(SSM_WIDTH), out_shape=jax.ShapeDtypeStruct((b, l, SSM_WIDTH), bf16),
        scratch_shapes=[pltpu.VMEM((n, SSM_CHUNK, SSM_CHUNK), f32),
                        pltpu.VMEM((n, DT_WIDTH, SSM_CHUNK), f32),
                        pltpu.VMEM((l, SSM_WIDTH), f32),
                        pltpu.VMEM((n, SSM_STATE, SSM_WIDTH), f32),
                        pltpu.VMEM((SSM_STATE, SSM_WIDTH), f32), pltpu.VMEM((SSM_STATE, SSM_WIDTH), f32)],
        compiler_params=_cparams("parallel"), name="ssd_bidir",
    )(xbc, dtt, dtb, alog, expand, gate, dskip)


def _ssd_bidir_expand_matrix():
    row = jnp.arange(SSM_CHUNK, dtype=jnp.int32)[:, None]
    lane = jnp.arange(4 * SSM_WIDTH, dtype=jnp.int32)[None, :]
    head = (lane % SSM_WIDTH) // SSM_HEAD_DIM
    quantity = (lane // SSM_WIDTH) % 2
    direction = lane // (2 * SSM_WIDTH)
    base = direction * SSD_ROWS_PER_DIR + SSM_HEADS + 2 * SSM_HEADS * quantity
    hit = jnp.logical_or(row == base + head, row == base + SSM_HEADS + head)
    return hit.astype(bf16)


def _kv_kernel(mem_ref, wkv_ref, k_ref, v_ref):
    mn = _rms(mem_ref[0]).astype(bf16)
    k_ref[0] = _dot(mn, wkv_ref[:, :D_MODEL]).astype(bf16)
    v_ref[0] = _dot(mn, wkv_ref[:, D_MODEL:]).astype(bf16)


def _kv_proj(mem, wkv, layer):
    b, m, d = mem.shape
    spec = pl.BlockSpec((1, m, d), lambda bi: (bi, 0, 0))
    return pl.pallas_call(
        _kv_kernel, grid=(b,), in_specs=[spec, _layer_spec(wkv, layer)],
        out_specs=(spec, spec), out_shape=(jax.ShapeDtypeStruct((b, m, d), bf16),) * 2,
        compiler_params=_cparams("parallel"), name="kv_proj",
    )(mem, wkv)


def _mix_attn_kernel(x_ref, x0_ref, z_ref, zc_ref, ys_ref, hb_ref, wout_ref, wq_ref,
                     k_ref, v_ref, wo_ref, o_ref):
    z = jnp.concatenate([z_ref[0, s] for s in range(HY_SLABS)], axis=1)
    zc = jnp.concatenate([zc_ref[0, s] for s in range(HY_SLABS)], axis=1)
    y_hy = _rms(x0_ref[0].astype(f32) * (zc + z * hb_ref[...])).astype(bf16)
    x1 = x_ref[0] + _dot(jnp.concatenate([y_hy, ys_ref[0]], axis=1), wout_ref[...])
    xn = _rms(x1).astype(bf16)
    qa = _dot(xn, wq_ref[...]).astype(bf16)
    heads = []
    for h in range(XA_HEADS):
        hs = slice(h * XA_HEAD_DIM, (h + 1) * XA_HEAD_DIM)
        s = _dot_nt(qa[:, hs], k_ref[0, :, hs])
        e = jnp.exp(s - jnp.max(s, axis=-1, keepdims=True))
        den = jnp.sum(e, axis=-1, keepdims=True)
        heads.append(_dot(e.astype(bf16), v_ref[0, :, hs]) / den)
    o = jnp.concatenate(heads, axis=1).astype(bf16)
    o_ref[0] = x1 + _dot(o, wo_ref[...])


def _mix_attn(x, x0, z4, zc4, ys, hbias, wout, wq, k, v, wo, layer):
    b, l, d = x.shape
    tm = ROW_TILE
    row = lambda w: pl.BlockSpec((1, tm, w), lambda bi, i: (bi, i, 0))
    slab = pl.BlockSpec((1, HY_SLABS, tm, LANES), lambda bi, i: (bi, 0, i, 0))
    memspec = pl.BlockSpec((1, N_MEM, d), lambda bi, i: (bi, 0, 0))
    return pl.pallas_call(
        _mix_attn_kernel, grid=(b, l // tm),
        in_specs=[row(d), row(HY_WIDTH), slab, slab, row(SSM_WIDTH), _layer_spec(hbias, layer),
                  _layer_spec(wout, layer), _layer_spec(wq, layer), memspec, memspec, _layer_spec(wo, layer)],
        out_specs=row(d), out_shape=jax.ShapeDtypeStruct((b, l, d), f32),
        compiler_params=_cparams("parallel", "parallel"), name="mix_attn",
    )(x, x0, z4, zc4, ys, hbias, wout, wq, k, v, wo)


def _mlp_kernel(x_ref, wup_ref, wdn_ref, fw_ref, o_ref, *, final):
    x = x_ref[0]
    xn = _rms(x).astype(bf16)
    acc = x
    for c in range(D_FF // D_MODEL):
        cs = slice(c * D_MODEL, (c + 1) * D_MODEL)
        hdn = jnp.maximum(_dot(xn, wup_ref[:, cs]), 0.0)
        acc = acc + _dot((hdn * hdn).astype(bf16), wdn_ref[cs, :])
    o_ref[0] = _rms(acc, fw_ref[...]) if final else acc


def _mlp(x, wup, wdn, fw, final, layer):
    b, l, d = x.shape
    tm = ROW_TILE
    row = pl.BlockSpec((1, tm, d), lambda bi, i: (bi, i, 0))
    wspec = lambda a: _layer_spec(a, layer, pl.Buffered(1))
    return pl.pallas_call(
        functools.partial(_mlp_kernel, final=final), grid=(b, l // tm),
        in_specs=[row, wspec(wup), wspec(wdn), _const_spec(fw.shape)],
        out_specs=row, out_shape=jax.ShapeDtypeStruct((b, l, d), f32),
        compiler_params=_cparams("parallel", "parallel"), name="mlp",
    )(x, wup, wdn, fw)


def _trunk(x, mem, p, tables):
    tc, ts, ic, isn = tables
    seq_len = x.shape[1]
    for i in range(DEPTH):
        x0, z4, gate, xbc, dtt = _inproj(x, p["w_main"], p["w_dtt"], p["hy_conv_w"], p["hy_conv_b"],
                                         p["ssm_conv_w"], p["ssm_conv_b"], layer=i)
        kre, kim = _filter_spectra(seq_len, p["filter_mlp"], i, tc, ts)
        zc4 = _longconv(z4, kre, kim, tc, ts, ic, isn)
        ys = _ssd_bidir(xbc, dtt, gate, p["ssm_dt_bias"], p["ssm_A_log"], p["ssd_expand"], p["ssm_D_wide"], layer=i)
        k, v = _kv_proj(mem, p["w_kv"], layer=i)
        x = _mix_attn(x, x0, z4, zc4, ys, p["hy_bias"], p["w_out"], p["w_q"], k, v, p["w_o"], layer=i)
        x = _mlp(x, p["w_up"], p["w_down"], p["norm_final"], final=(i == DEPTH - 1), layer=i)
    return x


def kernel(x_prompt, x_sample, mem_prompt, mem_sample, norm_mix, w_in, hy_conv_w, hy_conv_b, hy_fw1, hy_fb1, hy_fw2, hy_fb2, hy_fw3, hy_fb3, hy_sin_freq, hy_bias, hy_norm, ssm_conv_w, ssm_conv_b, ssm_dt_bias, ssm_A_log, ssm_D, ssm_norm, w_out, norm_xattn, norm_mem, w_q, w_kv, w_o, norm_mlp, w_up, w_down, norm_final):
    gained = lambda g, w: (g[..., None] * w).astype(bf16)
    rows = lambda a: a[:, None, :]
    w_main = gained(norm_mix, w_in)
    p = dict(norm_final=norm_final.reshape(1, -1), ssd_expand=_ssd_bidir_expand_matrix(),
             hy_conv_w=hy_conv_w, hy_conv_b=rows(hy_conv_b), ssm_conv_w=ssm_conv_w, ssm_conv_b=rows(ssm_conv_b),
             hy_bias=rows(hy_bias), ssm_dt_bias=ssm_dt_bias.reshape(DEPTH, DT_WIDTH, 1),
             ssm_A_log=ssm_A_log.reshape(DEPTH, DT_WIDTH, 1), ssm_D_wide=rows(jnp.repeat(ssm_D, SSM_HEAD_DIM, axis=-1)),
             filter_mlp=_filter_mlp_params(hy_fw1, hy_fb1, hy_fw2, hy_fb2, hy_fw3, hy_fb3, hy_sin_freq),
             w_main=w_main, w_dtt=jnp.swapaxes(w_main[:, :, MAIN_IN:], 1, 2),
             w_out=gained(jnp.concatenate([hy_norm, ssm_norm], axis=-1), w_out),
             w_q=gained(norm_xattn * (XA_HEAD_DIM ** -0.5), w_q), w_kv=gained(norm_mem, w_kv),
             w_o=w_o.astype(bf16), w_up=gained(norm_mlp, w_up), w_down=w_down.astype(bf16))
    tables = _dft_tables()
    return (_trunk(x_prompt, mem_prompt, p, tables), _trunk(x_sample, mem_sample, p, tables))
```

```python
import functools
import math

import jax
import jax.numpy as jnp
from jax import lax
from jax.experimental import pallas as pl
from jax.experimental.pallas import tpu as pltpu

f32 = jnp.float32
bf16 = jnp.bfloat16

D_MODEL = 1024
DEPTH = 2
N_MEM = 256
HY_WIDTH = 512
HY_CONV = 3
HY_BANDS = 16
HY_DECAY_TARGET = 1e-2
HY_FAST_PCT = 0.3
HY_SLOW_PCT = 1.5
SSM_WIDTH = 512
SSM_HEAD_DIM = 64
SSM_HEADS = 8
SSM_GROUPS = 2
SSM_STATE = 128
SSM_CONV = 5
SSM_CHUNK = 128
BC_WIDTH = SSM_GROUPS * SSM_STATE
XBC_WIDTH = SSM_WIDTH + 2 * BC_WIDTH
HY_IN = 3 * HY_WIDTH
MAIN_IN = HY_IN + SSM_WIDTH + XBC_WIDTH
DT_WIDTH = 2 * SSM_HEADS
XA_HEADS = 4
XA_HEAD_DIM = D_MODEL // XA_HEADS
D_FF = 4 * D_MODEL
EPS = 1e-5

LANES = 128
SUBLANES = 8
HALO = SUBLANES
SUB_LEN = 512
SUB_FFT = 2 * SUB_LEN
HY_SLABS = HY_WIDTH // LANES
ROW_TILE = 1024
PW_ROWS = 32
VMEM_LIMIT = 56 * 1024 * 1024


def _cparams(*sem):
    return pltpu.CompilerParams(dimension_semantics=sem, vmem_limit_bytes=VMEM_LIMIT)


def _rms(x, w=None):
    y = x * lax.rsqrt(jnp.mean(x * x, axis=-1, keepdims=True) + EPS)
    return y if w is None else y * w


def _dot(a, b):
    return jnp.dot(a, b, preferred_element_type=f32)


def _dot_hi(a, b):
    return jnp.dot(a, b, preferred_element_type=f32, precision=lax.Precision.HIGHEST)


def _dot_nt(a, b):
    return lax.dot_general(a, b, (((1,), (1,)), ((), ())), preferred_element_type=f32)


def _dot_tn(a, b):
    return lax.dot_general(a, b, (((0,), (0,)), ((), ())), preferred_element_type=f32)


def _split3(x):
    hi = x.astype(bf16)
    r1 = x - hi.astype(f32)
    mid = r1.astype(bf16)
    lo = (r1 - mid.astype(f32)).astype(bf16)
    return hi, mid, lo


def _silu(x):
    return x / (1.0 + jnp.exp(-x))


def _softplus(x):
    return jnp.maximum(x, 0.0) + jnp.log(1.0 + jnp.exp(-jnp.abs(x)))


def _const_spec(shape):
    nd = len(shape)
    return pl.BlockSpec(shape, lambda *_: (0,) * nd)


def _layer_spec(stacked, layer, pipeline_mode=None):
    nd = stacked.ndim
    kwargs = {} if pipeline_mode is None else dict(pipeline_mode=pipeline_mode)
    return pl.BlockSpec((None,) + stacked.shape[1:], lambda *_: (layer,) + (0,) * (nd - 1), **kwargs)


def _inproj_kernel(xp_ref, x_ref, xn_ref, w_ref, wdtt_ref, hcw_ref, hcb_ref,
                   scw_ref, scb_ref, x0_ref, z_ref, gate_ref, xbc_ref, dtt_ref, pbuf):
    tm = x_ref.shape[1]
    i = pl.program_id(1)
    xp = jnp.where(i > 0, _rms(xp_ref[0]), 0.0)
    xn = jnp.where(i < pl.num_programs(1) - 1, _rms(xn_ref[0]), 0.0)
    xm = _rms(x_ref[0])
    xe = jnp.concatenate([xp, xm, xn], axis=0).astype(bf16)
    xmb = xe[HALO:HALO + tm]

    def conv_slabs(buf, col0, cw_ref, cb_ref, taps, wcol0):
        p = _dot(xe, w_ref[:, col0:col0 + HY_WIDTH])
        for s in range(HY_SLABS):
            pbuf[buf, s] = p[:, s * LANES:(s + 1) * LANES]
        pad = taps // 2
        outs = []
        for s in range(HY_SLABS):
            cs = slice(wcol0 + s * LANES, wcol0 + (s + 1) * LANES)
            acc = cb_ref[:, cs]
            for k in range(taps):
                acc = acc + pbuf[buf, s, pl.ds(HALO + k - pad, tm), :] * cw_ref[k:k + 1, cs]
            outs.append(acc)
        return outs

    for s, blk in enumerate(conv_slabs(0, 0, hcw_ref, hcb_ref, HY_CONV, 0)):
        x0_ref[0, :, s * LANES:(s + 1) * LANES] = blk.astype(x0_ref.dtype)
    for s, blk in enumerate(conv_slabs(1, HY_WIDTH, hcw_ref, hcb_ref, HY_CONV, HY_WIDTH)):
        z_ref[0, s] = blk
    for s, blk in enumerate(conv_slabs(0, 2 * HY_WIDTH, hcw_ref, hcb_ref, HY_CONV, 2 * HY_WIDTH)):
        z_ref[0, s] = z_ref[0, s] * blk
    gate_ref[0] = _dot(xmb, w_ref[:, HY_IN:HY_IN + SSM_WIDTH]).astype(gate_ref.dtype)
    o2 = HY_IN + SSM_WIDTH
    for c in range(XBC_WIDTH // HY_WIDTH):
        for s, blk in enumerate(conv_slabs(1 - c % 2, o2 + c * HY_WIDTH, scw_ref, scb_ref, SSM_CONV, c * HY_WIDTH)):
            xbc_ref[0, :, c * HY_WIDTH + s * LANES:c * HY_WIDTH + (s + 1) * LANES] = _silu(blk).astype(xbc_ref.dtype)
    dtt_ref[0] = _dot_nt(wdtt_ref[...], xmb)


def _inproj(x, w_main, w_dtt, hcw, hcb, scw, scb, layer):
    b, l, d = x.shape
    tm = ROW_TILE
    nt = l // tm
    hb = tm // HALO
    grid = (b, nt)
    out_shape = (
        jax.ShapeDtypeStruct((b, l, HY_WIDTH), bf16),
        jax.ShapeDtypeStruct((b, HY_SLABS, l, LANES), f32),
        jax.ShapeDtypeStruct((b, l, SSM_WIDTH), bf16),
        jax.ShapeDtypeStruct((b, l, XBC_WIDTH), bf16),
        jax.ShapeDtypeStruct((b, DT_WIDTH, l), f32),
    )
    in_specs = [
        pl.BlockSpec((1, HALO, d), lambda bi, i: (bi, jnp.maximum(i * hb - 1, 0), 0)),
        pl.BlockSpec((1, tm, d), lambda bi, i: (bi, i, 0)),
        pl.BlockSpec((1, HALO, d), lambda bi, i: (bi, jnp.minimum((i + 1) * hb, l // HALO - 1), 0)),
        _layer_spec(w_main, layer),
        _layer_spec(w_dtt, layer), _layer_spec(hcw, layer), _layer_spec(hcb, layer),
        _layer_spec(scw, layer), _layer_spec(scb, layer),
    ]
    out_specs = (
        pl.BlockSpec((1, tm, HY_WIDTH), lambda bi, i: (bi, i, 0)),
        pl.BlockSpec((1, HY_SLABS, tm, LANES), lambda bi, i: (bi, 0, i, 0)),
        pl.BlockSpec((1, tm, SSM_WIDTH), lambda bi, i: (bi, i, 0)),
        pl.BlockSpec((1, tm, XBC_WIDTH), lambda bi, i: (bi, i, 0)),
        pl.BlockSpec((1, DT_WIDTH, tm), lambda bi, i: (bi, 0, i)),
    )
    return pl.pallas_call(
        _inproj_kernel, grid=grid, in_specs=in_specs, out_specs=out_specs, out_shape=out_shape,
        scratch_shapes=[pltpu.VMEM((2, HY_SLABS, tm + 2 * HALO, LANES), f32)],
        compiler_params=_cparams("parallel", "parallel"), name="inproj",
    )(x, x, x, w_main, w_dtt, hcw, hcb, scw, scb)


def _spectra_kernel(fw1t_ref, fw1c_ref, fw1s_ref, fb1_ref, fw2_ref, fb2_ref, fw3_ref, fb3_ref,
                    freq_ref, tc_ref, ts_ref, kre_ref, kim_ref, *, seq_len, phases):
    d = pl.program_id(0)
    lane = lax.broadcasted_iota(jnp.int32, (1, 2 * SUB_LEN), 1)
    m = lane % SUB_LEN
    pos = jnp.abs(jnp.where(lane >= SUB_LEN, d - phases * m, d + phases * m)).astype(f32)
    t = pos * (1.0 / (seq_len - 1))
    w = pos * (2.0 * math.pi / seq_len)
    band = lax.broadcasted_iota(jnp.int32, (HY_BANDS, 1), 0).astype(f32)
    fband = 1e-4 + band * ((HY_BANDS - 1 - 1e-4) / (HY_BANDS - 1))
    ang = fband * w
    fr = freq_ref[...]
    pre = fw1t_ref[...] * t + _dot_hi(fw1c_ref[...], jnp.cos(ang)) - _dot_hi(fw1s_ref[...], jnp.sin(ang))
    h = jnp.sin(fr * (pre + fb1_ref[...]))
    h = jnp.sin(fr * (_dot_hi(fw2_ref[...], h) + fb2_ref[...]))
    o = _dot(fw3_ref[...].astype(bf16), h.astype(bf16)) + fb3_ref[...]
    max_decay = math.log(HY_DECAY_TARGET) / HY_FAST_PCT
    min_decay = math.log(HY_DECAY_TARGET) / HY_SLOW_PCT
    ch = lax.broadcasted_iota(jnp.int32, (HY_WIDTH, 1), 0).astype(f32)
    deltas = jnp.abs(min_decay + ch * ((max_decay - min_decay) / (HY_WIDTH - 1)))
    decay = jnp.exp(-deltas * t)
    fwd = o[:HY_WIDTH] * decay
    bwd = o[HY_WIDTH:] * decay
    lag0 = m[:, :SUB_LEN] == 0
    gp = fwd[:, :SUB_LEN]
    gm = jnp.where(lag0, 0.0, bwd[:, SUB_LEN:])
    kre_ref[0, 0] = _dot_nt(tc_ref[...], (gp + gm).astype(bf16))
    kim_ref[0, 0] = _dot_nt(ts_ref[...], (gp - gm).astype(bf16))
    gp = jnp.where(jnp.logical_and(lag0, d > 0), bwd[:, SUB_LEN:], fwd[:, SUB_LEN:])
    gm = jnp.where(lag0, 0.0, bwd[:, :SUB_LEN])
    kre_ref[1, 0] = _dot_nt(tc_ref[...], (gp + gm).astype(bf16))
    kim_ref[1, 0] = _dot_nt(ts_ref[...], (gp - gm).astype(bf16))


def _filter_spectra(seq_len, filter_params, layer, tc, ts):
    phases = seq_len // SUB_LEN
    out_shape = (jax.ShapeDtypeStruct((2, phases, SUB_LEN, HY_WIDTH), f32),) * 2
    spec = pl.BlockSpec((2, 1, SUB_LEN, HY_WIDTH), lambda k: (0, k, 0, 0))
    return pl.pallas_call(
        functools.partial(_spectra_kernel, seq_len=seq_len, phases=phases),
        grid=(phases,),
        in_specs=[*[_layer_spec(a, layer) for a in filter_params], _const_spec(tc.shape), _const_spec(ts.shape)],
        out_specs=(spec, spec),
        out_shape=out_shape, compiler_params=_cparams("parallel"), name="filter_spectra",
    )(*filter_params, tc, ts)


def _filter_mlp_params(fw1, fb1, fw2, fb2, fw3, fb3, freq):
    col = lambda a: a[:, :, None]
    t = lambda a: jnp.swapaxes(a, 1, 2)
    return (col(fw1[:, 0]), t(fw1[:, 1:1 + HY_BANDS]), t(fw1[:, 1 + HY_BANDS:]), col(fb1), t(fw2), col(fb2),
            t(fw3), col(fb3), col(freq))


def _dft_tables():
    f = jnp.arange(SUB_LEN, dtype=jnp.int32)[:, None]
    n = jnp.arange(SUB_LEN, dtype=jnp.int32)[None, :]
    ang = (((2 * f + 1) * n) % (2 * SUB_FFT)).astype(f32) * (math.pi / SUB_FFT)
    c, s = jnp.cos(ang), jnp.sin(ang)
    tc, ts = c.astype(bf16), (-s).astype(bf16)
    scale = 2.0 / SUB_FFT
    ic = (scale * c).T.astype(bf16)
    isn = (-scale * s).T.astype(bf16)
    return tc, ts, ic, isn


def _longconv_kernel(z_ref, kre_ref, kim_ref, tc_ref, ts_ref, ic_ref, is_ref, o_ref,
                     zre, zim, wre, wim, *, phases):
    slabs = z_ref.shape[1]
    for q in range(phases):
        zq = jnp.concatenate(
            [z_ref[0, s, pl.ds(q, SUB_LEN, stride=phases), :] for s in range(slabs)], axis=1).astype(bf16)
        zre[q] = _dot(tc_ref[...], zq)
        zim[q] = _dot(ts_ref[...], zq)

    def pointwise(p, slot):
        for r in range(SUB_LEN // PW_ROWS):
            rs = slice(r * PW_ROWS, (r + 1) * PW_ROWS)
            are = jnp.zeros((PW_ROWS, zre.shape[2]), f32)
            aim = jnp.zeros((PW_ROWS, zre.shape[2]), f32)
            for q in range(phases):
                kd = jnp.where(p >= q, p - q, phases + q - p)
                kr, ki = kre_ref[kd, rs, :], kim_ref[kd, rs, :]
                xr, xi = zre[q, rs, :], zim[q, rs, :]
                are = are + (kr * xr - ki * xi)
                aim = aim + (kr * xi + ki * xr)
            wre[slot, rs, :] = are.astype(bf16)
            wim[slot, rs, :] = aim.astype(bf16)

    def inverse(p, slot):
        yp = _dot(ic_ref[...], wre[slot]) + _dot(is_ref[...], wim[slot])
        for s in range(slabs):
            o_ref[0, s, pl.ds(p, SUB_LEN, stride=phases), :] = yp[:, s * LANES:(s + 1) * LANES]

    pointwise(0, 0)

    def body(p, carry):
        pointwise(p, p % 2)
        inverse(p - 1, (p - 1) % 2)
        return carry

    lax.fori_loop(1, phases, body, 0)
    inverse(phases - 1, (phases - 1) % 2)


def _longconv(z4, kre, kim, tcf, tsf, ic, isn):
    b, slabs, l, _ = z4.shape
    phases = l // SUB_LEN
    sb = 2
    cb = sb * LANES
    zspec = pl.BlockSpec((1, sb, l, LANES), lambda j, bi: (bi, j, 0, 0))
    spectra_bytes = 2 * (2 * phases) * SUB_LEN * cb * 4
    kspec = pl.BlockSpec((2 * phases, SUB_LEN, cb), lambda j, bi: (0, 0, j),
                         pipeline_mode=pl.Buffered(1 if 2 * spectra_bytes > VMEM_LIMIT // 2 else 2))
    tspec = _const_spec(tcf.shape)
    return pl.pallas_call(
        functools.partial(_longconv_kernel, phases=phases),
        grid=(slabs // sb, b),
        in_specs=[zspec, kspec, kspec, tspec, tspec, tspec, tspec], out_specs=zspec,
        out_shape=jax.ShapeDtypeStruct(z4.shape, f32),
        scratch_shapes=[pltpu.VMEM((phases, SUB_LEN, cb), f32), pltpu.VMEM((phases, SUB_LEN, cb), f32),
                        pltpu.VMEM((2, SUB_LEN, cb), bf16), pltpu.VMEM((2, SUB_LEN, cb), bf16)],
        compiler_params=_cparams("parallel", "parallel"), name="longconv",
    )(z4, kre.reshape(2 * phases, SUB_LEN, HY_WIDTH), kim.reshape(2 * phases, SUB_LEN, HY_WIDTH),
      tcf, tsf, ic, isn)


SSD_ROWS_PER_DIR = 5 * SSM_HEADS
SSD_UNROLL = 8


def _ssd_bidir_kernel(xbc_ref, dtt_ref, dtb_ref, al_ref, ex_ref, gate_ref, dskip_ref, y_ref,
                      smat, rtmat, yacc, stash, hf_ref, hb_ref):
    q = SSM_CHUNK
    nh = SSM_HEADS
    n = xbc_ref.shape[1] // q
    half = n // 2
    rows_dir = SSD_ROWS_PER_DIR
    hp = (nh // SSM_GROUPS) * SSM_HEAD_DIM
    log2e = 1.0 / math.log(2.0)

    ii = lax.broadcasted_iota(jnp.int32, (q, q), 0)
    jj = lax.broadcasted_iota(jnp.int32, (q, q), 1)
    lower, upper = ii >= jj, ii <= jj
    first_head = lax.broadcasted_iota(jnp.int32, (q, LANES), 1) < SSM_HEAD_DIM

    dt16 = _softplus(dtt_ref[0] + dtb_ref[...])
    da16 = dt16 * (-jnp.exp(al_ref[...]))
    stack = lambda a: jnp.concatenate([a[:, c * q:(c + 1) * q] for c in range(n)], axis=0)
    smat[:, 2 * rows_dir:, :] = jnp.zeros((n, q - 2 * rows_dir, q), f32)
    for d in range(2):
        m_sum = jnp.where(upper if d == 0 else lower, 1.0, 0.0).astype(bf16)
        dt = stack(dt16[d * nh:(d + 1) * nh])
        cs2 = sum(_dot(part, m_sum) for part in _split3(stack(da16[d * nh:(d + 1) * nh]))) * log2e
        edge = cs2[:, q - 1:q] if d == 0 else cs2[:, 0:1]
        e = jnp.exp2(cs2)
        dd = dt * jnp.exp2(edge - cs2)
        e_hi = e.astype(bf16).astype(f32)
        dd_hi = dd.astype(bf16).astype(f32)
        for j, a in enumerate((cs2, e_hi, e - e_hi, dd_hi, dd - dd_hi)):
            smat[:, d * rows_dir + j * nh:d * rows_dir + (j + 1) * nh, :] = a.reshape(n, nh, q)
        rtmat[:, d * nh:(d + 1) * nh, :] = (cs2 - jnp.log2(dt)).reshape(n, nh, q)
    hf_ref[...] = jnp.zeros_like(hf_ref)
    hb_ref[...] = jnp.zeros_like(hb_ref)

    def chunk_rows(c):
        return pl.ds(pl.multiple_of(c * q, q), q)

    def in_chunk(c, d, h_ref):
        rows = chunk_rows(c)
        xsb = xbc_ref[0, rows, 0:SSM_WIDTH]
        bm = xbc_ref[0, rows, SSM_WIDTH:SSM_WIDTH + BC_WIDTH]
        cm = xbc_ref[0, rows, SSM_WIDTH + BC_WIDTH:XBC_WIDTH]
        xs = xsb.astype(f32)
        cols = smat[c].T
        wide = _dot(cols.astype(bf16), ex_ref[...])
        e_f, dd_f = wide[:, 0:SSM_WIDTH], wide[:, SSM_WIDTH:2 * SSM_WIDTH]
        e_b, dd_b = wide[:, 2 * SSM_WIDTH:3 * SSM_WIDTH], wide[:, 3 * SSM_WIDTH:]
        xdd_f = (xs * dd_f).astype(bf16)
        xdd_b = (xs * dd_b).astype(bf16)
        rt = rtmat[c]
        cstack = jnp.concatenate([cm[:, :SSM_STATE], cm[:, SSM_STATE:]], axis=0)
        bstack = jnp.concatenate([bm[:, :SSM_STATE], bm[:, SSM_STATE:]], axis=0)
        cb_all = _dot_nt(cstack, bstack)
        ys, sts_f, sts_b = [], [], []
        for g in range(SSM_GROUPS):
            bg = bm[:, g * SSM_STATE:(g + 1) * SSM_STATE]
            gl = slice(g * hp, (g + 1) * hp)
            cb = cb_all[g * q:(g + 1) * q, g * q:(g + 1) * q]
            st = _dot_tn(bg, jnp.concatenate([xdd_f[:, gl], xdd_b[:, gl]], axis=1))
            sts_f.append(st[:, :hp])
            sts_b.append(st[:, hp:])
            for pr in range(nh // SSM_GROUPS // 2):
                ms = []
                for hd in (g * (nh // SSM_GROUPS) + 2 * pr, g * (nh // SSM_GROUPS) + 2 * pr + 1):
                    df = cols[:, hd:hd + 1] - rt[hd:hd + 1, :]
                    db = cols[:, rows_dir + hd:rows_dir + hd + 1] - rt[nh + hd:nh + hd + 1, :]
                    lm = jnp.exp2(jnp.where(lower, df, -1e30)) + jnp.exp2(jnp.where(upper, db, -1e30))
                    ms.append((cb * lm).astype(bf16))
                xp = xsb[:, g * hp + pr * LANES:g * hp + (pr + 1) * LANES]
                zero = jnp.zeros_like(xp)
                rhs = jnp.concatenate([jnp.where(first_head, xp, zero), jnp.where(first_head, zero, xp)], axis=0)
                ys.append(_dot(jnp.concatenate(ms, axis=1), rhs))
        st = (jnp.concatenate(sts_f, axis=1), jnp.concatenate(sts_b, axis=1))
        yoff = carried(c, h_ref, (e_f, e_b)[d], st[d], (q - 1, 0)[d])
        yacc[rows, :] = jnp.concatenate(ys, axis=1) + xs * dskip_ref[...] + yoff
        return st[1 - d]

    def expand_e(c, d):
        cols = smat[c].T
        return _dot(cols.astype(bf16), ex_ref[:, 2 * d * SSM_WIDTH:(2 * d + 1) * SSM_WIDTH])

    def carried(c, h_ref, e_x, st, edge_row):
        cm = xbc_ref[0, chunk_rows(c), SSM_WIDTH + BC_WIDTH:XBC_WIDTH]
        h = h_ref[...]
        hb = h.astype(bf16)
        zh = jnp.zeros((SSM_STATE, hp), bf16)
        h_blocks = jnp.concatenate([jnp.concatenate([hb[:, :hp], zh], axis=1),
                                    jnp.concatenate([zh, hb[:, hp:]], axis=1)], axis=0)
        h_ref[...] = h * e_x[edge_row:edge_row + 1, :] + st
        return _dot(cm, h_blocks) * e_x

    def finish(c, y):
        rows = chunk_rows(c)
        gated = y * _silu(gate_ref[0, rows, :].astype(f32))
        y_ref[0, rows, :] = _rms(gated).astype(y_ref.dtype)

    def first_half(k, carry):
        m = n - 1 - k
        stash[k] = in_chunk(k, 0, hf_ref)
        stash[m] = in_chunk(m, 1, hb_ref)
        return carry

    def second_half(k, carry):
        m = n - 1 - k
        finish(k, yacc[chunk_rows(k), :] + carried(k, hf_ref, expand_e(k, 0), stash[k], q - 1))
        finish(m, yacc[chunk_rows(m), :] + carried(m, hb_ref, expand_e(m, 1), stash[m], 0))
        return carry

    lax.fori_loop(0, half, first_half, 0, unroll=SSD_UNROLL)
    lax.fori_loop(half, n, second_half, 0, unroll=SSD_UNROLL)


def _ssd_bidir(xbc, dtt, gate, dtb, alog, expand, dskip, layer):
    b, l, _ = xbc.shape
    n = l // SSM_CHUNK
    seq = lambda w: pl.BlockSpec((1, l, w), lambda bi: (bi, 0, 0))
    return pl.pallas_call(
        _ssd_bidir_kernel, grid=(b,),
        in_specs=[seq(XBC_WIDTH), pl.BlockSpec((1, DT_WIDTH, l), lambda bi: (bi, 0, 0)),
                  _layer_spec(dtb, layer), _layer_spec(alog, layer), _const_spec(expand.shape), seq(SSM_WIDTH),
                  _layer_spec(dskip, layer)],
        out_specs=seq(SSM_WIDTH), out_shape=jax.ShapeDtypeStruct((b, l, SSM_WIDTH), bf16),
        scratch_shapes=[pltpu.VMEM((n, SSM_CHUNK, SSM_CHUNK), f32),
                        pltpu.VMEM((n, DT_WIDTH, SSM_CHUNK), f32),
                        pltpu.VMEM((l, SSM_WIDTH), f32),
                        pltpu.VMEM((n, SSM_STATE, SSM_WIDTH), f32),
                        pltpu.VMEM((SSM_STATE, SSM_WIDTH), f32), pltpu.VMEM((SSM_STATE, SSM_WIDTH), f32)],
        compiler_params=_cparams("parallel"), name="ssd_bidir",
    )(xbc, dtt, dtb, alog, expand, gate, dskip)


def _ssd_bidir_expand_matrix():
    row = jnp.arange(SSM_CHUNK, dtype=jnp.int32)[:, None]
    lane = jnp.arange(4 * SSM_WIDTH, dtype=jnp.int32)[None, :]
    head = (lane % SSM_WIDTH) // SSM_HEAD_DIM
    quantity = (lane // SSM_WIDTH) % 2
    direction = lane // (2 * SSM_WIDTH)
    base = direction * SSD_ROWS_PER_DIR + SSM_HEADS + 2 * SSM_HEADS * quantity
    hit = jnp.logical_or(row == base + head, row == base + SSM_HEADS + head)
    return hit.astype(bf16)


def _kv_kernel(mem_ref, wkv_ref, k_ref, v_ref):
    mn = _rms(mem_ref[0]).astype(bf16)
    k_ref[0] = _dot(mn, wkv_ref[:, :D_MODEL]).astype(bf16)
    v_ref[0] = _dot(mn, wkv_ref[:, D_MODEL:]).astype(bf16)


def _kv_proj(mem, wkv, layer):
    b, m, d = mem.shape
    spec = pl.BlockSpec((1, m, d), lambda bi: (bi, 0, 0))
    return pl.pallas_call(
        _kv_kernel, grid=(b,), in_specs=[spec, _layer_spec(wkv, layer)],
        out_specs=(spec, spec), out_shape=(jax.ShapeDtypeStruct((b, m, d), bf16),) * 2,
        compiler_params=_cparams("parallel"), name="kv_proj",
    )(mem, wkv)


def _mix_attn_kernel(x_ref, x0_ref, z_ref, zc_ref, ys_ref, hb_ref, wout_ref, wq_ref,
                     k_ref, v_ref, wo_ref, o_ref):
    z = jnp.concatenate([z_ref[0, s] for s in range(HY_SLABS)], axis=1)
    zc = jnp.concatenate([zc_ref[0, s] for s in range(HY_SLABS)], axis=1)
    y_hy = _rms(x0_ref[0].astype(f32) * (zc + z * hb_ref[...])).astype(bf16)
    x1 = x_ref[0] + _dot(jnp.concatenate([y_hy, ys_ref[0]], axis=1), wout_ref[...])
    xn = _rms(x1).astype(bf16)
    qa = _dot(xn, wq_ref[...]).astype(bf16)
    heads = []
    for h in range(XA_HEADS):
        hs = slice(h * XA_HEAD_DIM, (h + 1) * XA_HEAD_DIM)
        s = _dot_nt(qa[:, hs], k_ref[0, :, hs])
        e = jnp.exp(s - jnp.max(s, axis=-1, keepdims=True))
        den = jnp.sum(e, axis=-1, keepdims=True)
        heads.append(_dot(e.astype(bf16), v_ref[0, :, hs]) / den)
    o = jnp.concatenate(heads, axis=1).astype(bf16)
    o_ref[0] = x1 + _dot(o, wo_ref[...])


def _mix_attn(x, x0, z4, zc4, ys, hbias, wout, wq, k, v, wo, layer):
    b, l, d = x.shape
    tm = ROW_TILE
    row = lambda w: pl.BlockSpec((1, tm, w), lambda bi, i: (bi, i, 0))
    slab = pl.BlockSpec((1, HY_SLABS, tm, LANES), lambda bi, i: (bi, 0, i, 0))
    memspec = pl.BlockSpec((1, N_MEM, d), lambda bi, i: (bi, 0, 0))
    return pl.pallas_call(
        _mix_attn_kernel, grid=(b, l // tm),
        in_specs=[row(d), row(HY_WIDTH), slab, slab, row(SSM_WIDTH), _layer_spec(hbias, layer),
                  _layer_spec(wout, layer), _layer_spec(wq, layer), memspec, memspec, _layer_spec(wo, layer)],
        out_specs=row(d), out_shape=jax.ShapeDtypeStruct((b, l, d), f32),
        compiler_params=_cparams("parallel", "parallel"), name="mix_attn",
    )(x, x0, z4, zc4, ys, hbias, wout, wq, k, v, wo)


def _mlp_kernel(x_ref, wup_ref, wdn_ref, fw_ref, o_ref, *, final):
    x = x_ref[0]
    xn = _rms(x).astype(bf16)
    acc = x
    for c in range(D_FF // D_MODEL):
        cs = slice(c * D_MODEL, (c + 1) * D_MODEL)
        hdn = jnp.maximum(_dot(xn, wup_ref[:, cs]), 0.0)
        acc = acc + _dot((hdn * hdn).astype(bf16), wdn_ref[cs, :])
    o_ref[0] = _rms(acc, fw_ref[...]) if final else acc


def _mlp(x, wup, wdn, fw, final, layer):
    b, l, d = x.shape
    tm = ROW_TILE
    row = pl.BlockSpec((1, tm, d), lambda bi, i: (bi, i, 0))
    wspec = lambda a: _layer_spec(a, layer, pl.Buffered(1))
    return pl.pallas_call(
        functools.partial(_mlp_kernel, final=final), grid=(b, l // tm),
        in_specs=[row, wspec(wup), wspec(wdn), _const_spec(fw.shape)],
        out_specs=row, out_shape=jax.ShapeDtypeStruct((b, l, d), f32),
        compiler_params=_cparams("parallel", "parallel"), name="mlp",
    )(x, wup, wdn, fw)


def _trunk(x, mem, p, tables):
    tc, ts, ic, isn = tables
    seq_len = x.shape[1]
    for i in range(DEPTH):
        x0, z4, gate, xbc, dtt = _inproj(x, p["w_main"], p["w_dtt"], p["hy_conv_w"], p["hy_conv_b"],
                                         p["ssm_conv_w"], p["ssm_conv_b"], layer=i)
        kre, kim = _filter_spectra(seq_len, p["filter_mlp"], i, tc, ts)
        zc4 = _longconv(z4, kre, kim, tc, ts, ic, isn)
        ys = _ssd_bidir(xbc, dtt, gate, p["ssm_dt_bias"], p["ssm_A_log"], p["ssd_expand"], p["ssm_D_wide"], layer=i)
        k, v = _kv_proj(mem, p["w_kv"], layer=i)
        x = _mix_attn(x, x0, z4, zc4, ys, p["hy_bias"], p["w_out"], p["w_q"], k, v, p["w_o"], layer=i)
        x = _mlp(x, p["w_up"], p["w_down"], p["norm_final"], final=(i == DEPTH - 1), layer=i)
    return x


def kernel(x_prompt, x_sample, mem_prompt, mem_sample, norm_mix, w_in, hy_conv_w, hy_conv_b, hy_fw1, hy_fb1, hy_fw2, hy_fb2, hy_fw3, hy_fb3, hy_sin_freq, hy_bias, hy_norm, ssm_conv_w, ssm_conv_b, ssm_dt_bias, ssm_A_log, ssm_D, ssm_norm, w_out, norm_xattn, norm_mem, w_q, w_kv, w_o, norm_mlp, w_up, w_down, norm_final):
    gained = lambda g, w: (g[..., None] * w).astype(bf16)
    rows = lambda a: a[:, None, :]
    w_main = gained(norm_mix, w_in)
    p = dict(norm_final=norm_final.reshape(1, -1), ssd_expand=_ssd_bidir_expand_matrix(),
             hy_conv_w=hy_conv_w, hy_conv_b=rows(hy_conv_b), ssm_conv_w=ssm_conv_w, ssm_conv_b=rows(ssm_conv_b),
             hy_bias=rows(hy_bias), ssm_dt_bias=ssm_dt_bias.reshape(DEPTH, DT_WIDTH, 1),
             ssm_A_log=ssm_A_log.reshape(DEPTH, DT_WIDTH, 1), ssm_D_wide=rows(jnp.repeat(ssm_D, SSM_HEAD_DIM, axis=-1)),
             filter_mlp=_filter_mlp_params(hy_fw1, hy_fb1, hy_fw2, hy_fb2, hy_fw3, hy_fb3, hy_sin_freq),
             w_main=w_main, w_dtt=jnp.swapaxes(w_main[:, :, MAIN_IN:], 1, 2),
             w_out=gained(jnp.concatenate([hy_norm, ssm_norm], axis=-1), w_out),
             w_q=gained(norm_xattn * (XA_HEAD_DIM ** -0.5), w_q), w_kv=gained(norm_mem, w_kv),
             w_o=w_o.astype(bf16), w_up=gained(norm_mlp, w_up), w_down=w_down.astype(bf16))
    tables = _dft_tables()
    return (_trunk(x_prompt, mem_prompt, p, tables), _trunk(x_sample, mem_sample, p, tables))
```

```python
import functools
import math

import jax
import jax.numpy as jnp
from jax import lax
from jax.experimental import pallas as pl
from jax.experimental.pallas import tpu as pltpu

f32 = jnp.float32
bf16 = jnp.bfloat16

D_MODEL = 1024
DEPTH = 2
N_MEM = 256
HY_WIDTH = 512
HY_CONV = 3
HY_BANDS = 16
HY_DECAY_TARGET = 1e-2
HY_FAST_PCT = 0.3
HY_SLOW_PCT = 1.5
SSM_WIDTH = 512
SSM_HEAD_DIM = 64
SSM_HEADS = 8
SSM_GROUPS = 2
SSM_STATE = 128
SSM_CONV = 5
SSM_CHUNK = 128
BC_WIDTH = SSM_GROUPS * SSM_STATE
XBC_WIDTH = SSM_WIDTH + 2 * BC_WIDTH
HY_IN = 3 * HY_WIDTH
MAIN_IN = HY_IN + SSM_WIDTH + XBC_WIDTH
DT_WIDTH = 2 * SSM_HEADS
XA_HEADS = 4
XA_HEAD_DIM = D_MODEL // XA_HEADS
D_FF = 4 * D_MODEL
EPS = 1e-5

LANES = 128
SUBLANES = 8
HALO = SUBLANES
SUB_LEN = 512
SUB_FFT = 2 * SUB_LEN
HY_SLABS = HY_WIDTH // LANES
ROW_TILE = 1024
PW_ROWS = 32
VMEM_LIMIT = 56 * 1024 * 1024


def _cparams(*sem):
    return pltpu.CompilerParams(dimension_semantics=sem, vmem_limit_bytes=VMEM_LIMIT)


def _rms(x, w=None):
    y = x * lax.rsqrt(jnp.mean(x * x, axis=-1, keepdims=True) + EPS)
    return y if w is None else y * w


def _dot(a, b):
    return jnp.dot(a, b, preferred_element_type=f32)


def _dot_hi(a, b):
    return jnp.dot(a, b, preferred_element_type=f32, precision=lax.Precision.HIGHEST)


def _dot_nt(a, b):
    return lax.dot_general(a, b, (((1,), (1,)), ((), ())), preferred_element_type=f32)


def _dot_tn(a, b):
    return lax.dot_general(a, b, (((0,), (0,)), ((), ())), preferred_element_type=f32)


def _split3(x):
    hi = x.astype(bf16)
    r1 = x - hi.astype(f32)
    mid = r1.astype(bf16)
    lo = (r1 - mid.astype(f32)).astype(bf16)
    return hi, mid, lo


def _silu(x):
    return x / (1.0 + jnp.exp(-x))


def _softplus(x):
    return jnp.maximum(x, 0.0) + jnp.log(1.0 + jnp.exp(-jnp.abs(x)))


def _const_spec(shape):
    nd = len(shape)
    return pl.BlockSpec(shape, lambda *_: (0,) * nd)


def _layer_spec(stacked, layer, pipeline_mode=None):
    nd = stacked.ndim
    kwargs = {} if pipeline_mode is None else dict(pipeline_mode=pipeline_mode)
    return pl.BlockSpec((None,) + stacked.shape[1:], lambda *_: (layer,) + (0,) * (nd - 1), **kwargs)


def _inproj_kernel(xp_ref, x_ref, xn_ref, w_ref, wdtt_ref, hcw_ref, hcb_ref,
                   scw_ref, scb_ref, x0_ref, z_ref, gate_ref, xbc_ref, dtt_ref, pbuf):
    tm = x_ref.shape[1]
    i = pl.program_id(1)
    xp = jnp.where(i > 0, _rms(xp_ref[0]), 0.0)
    xn = jnp.where(i < pl.num_programs(1) - 1, _rms(xn_ref[0]), 0.0)
    xm = _rms(x_ref[0])
    xe = jnp.concatenate([xp, xm, xn], axis=0).astype(bf16)
    xmb = xe[HALO:HALO + tm]

    def conv_slabs(buf, col0, cw_ref, cb_ref, taps, wcol0):
        p = _dot(xe, w_ref[:, col0:col0 + HY_WIDTH])
        for s in range(HY_SLABS):
            pbuf[buf, s] = p[:, s * LANES:(s + 1) * LANES]
        pad = taps // 2
        outs = []
        for s in range(HY_SLABS):
            cs = slice(wcol0 + s * LANES, wcol0 + (s + 1) * LANES)
            acc = cb_ref[:, cs]
            for k in range(taps):
                acc = acc + pbuf[buf, s, pl.ds(HALO + k - pad, tm), :] * cw_ref[k:k + 1, cs]
            outs.append(acc)
        return outs

    for s, blk in enumerate(conv_slabs(0, 0, hcw_ref, hcb_ref, HY_CONV, 0)):
        x0_ref[0, :, s * LANES:(s + 1) * LANES] = blk.astype(x0_ref.dtype)
    for s, blk in enumerate(conv_slabs(1, HY_WIDTH, hcw_ref, hcb_ref, HY_CONV, HY_WIDTH)):
        z_ref[0, s] = blk
    for s, blk in enumerate(conv_slabs(0, 2 * HY_WIDTH, hcw_ref, hcb_ref, HY_CONV, 2 * HY_WIDTH)):
        z_ref[0, s] = z_ref[0, s] * blk
    gate_ref[0] = _dot(xmb, w_ref[:, HY_IN:HY_IN + SSM_WIDTH]).astype(gate_ref.dtype)
    o2 = HY_IN + SSM_WIDTH
    for c in range(XBC_WIDTH // HY_WIDTH):
        for s, blk in enumerate(conv_slabs(1 - c % 2, o2 + c * HY_WIDTH, scw_ref, scb_ref, SSM_CONV, c * HY_WIDTH)):
            xbc_ref[0, :, c * HY_WIDTH + s * LANES:c * HY_WIDTH + (s + 1) * LANES] = _silu(blk).astype(xbc_ref.dtype)
    dtt_ref[0] = _dot_nt(wdtt_ref[...], xmb)


def _inproj(x, w_main, w_dtt, hcw, hcb, scw, scb, layer):
    b, l, d = x.shape
    tm = ROW_TILE
    nt = l // tm
    hb = tm // HALO
    grid = (b, nt)
    out_shape = (
        jax.ShapeDtypeStruct((b, l, HY_WIDTH), bf16),
        jax.ShapeDtypeStruct((b, HY_SLABS, l, LANES), f32),
        jax.ShapeDtypeStruct((b, l, SSM_WIDTH), bf16),
        jax.ShapeDtypeStruct((b, l, XBC_WIDTH), bf16),
        jax.ShapeDtypeStruct((b, DT_WIDTH, l), f32),
    )
    in_specs = [
        pl.BlockSpec((1, HALO, d), lambda bi, i: (bi, jnp.maximum(i * hb - 1, 0), 0)),
        pl.BlockSpec((1, tm, d), lambda bi, i: (bi, i, 0)),
        pl.BlockSpec((1, HALO, d), lambda bi, i: (bi, jnp.minimum((i + 1) * hb, l // HALO - 1), 0)),
        _layer_spec(w_main, layer),
        _layer_spec(w_dtt, layer), _layer_spec(hcw, layer), _layer_spec(hcb, layer),
        _layer_spec(scw, layer), _layer_spec(scb, layer),
    ]
    out_specs = (
        pl.BlockSpec((1, tm, HY_WIDTH), lambda bi, i: (bi, i, 0)),
        pl.BlockSpec((1, HY_SLABS, tm, LANES), lambda bi, i: (bi, 0, i, 0)),
        pl.BlockSpec((1, tm, SSM_WIDTH), lambda bi, i: (bi, i, 0)),
        pl.BlockSpec((1, tm, XBC_WIDTH), lambda bi, i: (bi, i, 0)),
        pl.BlockSpec((1, DT_WIDTH, tm), lambda bi, i: (bi, 0, i)),
    )
    return pl.pallas_call(
        _inproj_kernel, grid=grid, in_specs=in_specs, out_specs=out_specs, out_shape=out_shape,
        scratch_shapes=[pltpu.VMEM((2, HY_SLABS, tm + 2 * HALO, LANES), f32)],
        compiler_params=_cparams("parallel", "parallel"), name="inproj",
    )(x, x, x, w_main, w_dtt, hcw, hcb, scw, scb)


def _spectra_kernel(fw1t_ref, fw1c_ref, fw1s_ref, fb1_ref, fw2_ref, fb2_ref, fw3_ref, fb3_ref,
                    freq_ref, tc_ref, ts_ref, kre_ref, kim_ref, *, seq_len, phases):
    d = pl.program_id(0)
    lane = lax.broadcasted_iota(jnp.int32, (1, 2 * SUB_LEN), 1)
    m = lane % SUB_LEN
    pos = jnp.abs(jnp.where(lane >= SUB_LEN, d - phases * m, d + phases * m)).astype(f32)
    t = pos * (1.0 / (seq_len - 1))
    w = pos * (2.0 * math.pi / seq_len)
    band = lax.broadcasted_iota(jnp.int32, (HY_BANDS, 1), 0).astype(f32)
    fband = 1e-4 + band * ((HY_BANDS - 1 - 1e-4) / (HY_BANDS - 1))
    ang = fband * w
    fr = freq_ref[...]
    pre = fw1t_ref[...] * t + _dot_hi(fw1c_ref[...], jnp.cos(ang)) - _dot_hi(fw1s_ref[...], jnp.sin(ang))
    h = jnp.sin(fr * (pre + fb1_ref[...]))
    h = jnp.sin(fr * (_dot_hi(fw2_ref[...], h) + fb2_ref[...]))
    o = _dot(fw3_ref[...].astype(bf16), h.astype(bf16)) + fb3_ref[...]
    max_decay = math.log(HY_DECAY_TARGET) / HY_FAST_PCT
    min_decay = math.log(HY_DECAY_TARGET) / HY_SLOW_PCT
    ch = lax.broadcasted_iota(jnp.int32, (HY_WIDTH, 1), 0).astype(f32)
    deltas = jnp.abs(min_decay + ch * ((max_decay - min_decay) / (HY_WIDTH - 1)))
    decay = jnp.exp(-deltas * t)
    fwd = o[:HY_WIDTH] * decay
    bwd = o[HY_WIDTH:] * decay
    lag0 = m[:, :SUB_LEN] == 0
    gp = fwd[:, :SUB_LEN]
    gm = jnp.where(lag0, 0.0, bwd[:, SUB_LEN:])
    kre_ref[0, 0] = _dot_nt(tc_ref[...], (gp + gm).astype(bf16))
    kim_ref[0, 0] = _dot_nt(ts_ref[...], (gp - gm).astype(bf16))
    gp = jnp.where(jnp.logical_and(lag0, d > 0), bwd[:, SUB_LEN:], fwd[:, SUB_LEN:])
    gm = jnp.where(lag0, 0.0, bwd[:, :SUB_LEN])
    kre_ref[1, 0] = _dot_nt(tc_ref[...], (gp + gm).astype(bf16))
    kim_ref[1, 0] = _dot_nt(ts_ref[...], (gp - gm).astype(bf16))


def _filter_spectra(seq_len, filter_params, layer, tc, ts):
    phases = seq_len // SUB_LEN
    out_shape = (jax.ShapeDtypeStruct((2, phases, SUB_LEN, HY_WIDTH), f32),) * 2
    spec = pl.BlockSpec((2, 1, SUB_LEN, HY_WIDTH), lambda k: (0, k, 0, 0))
    return pl.pallas_call(
        functools.partial(_spectra_kernel, seq_len=seq_len, phases=phases),
        grid=(phases,),
        in_specs=[*[_layer_spec(a, layer) for a in filter_params], _const_spec(tc.shape), _const_spec(ts.shape)],
        out_specs=(spec, spec),
        out_shape=out_shape, compiler_params=_cparams("parallel"), name="filter_spectra",
    )(*filter_params, tc, ts)


def _filter_mlp_params(fw1, fb1, fw2, fb2, fw3, fb3, freq):
    col = lambda a: a[:, :, None]
    t = lambda a: jnp.swapaxes(a, 1, 2)
    return (col(fw1[:, 0]), t(fw1[:, 1:1 + HY_BANDS]), t(fw1[:, 1 + HY_BANDS:]), col(fb1), t(fw2), col(fb2),
            t(fw3), col(fb3), col(freq))


def _dft_tables():
    f = jnp.arange(SUB_LEN, dtype=jnp.int32)[:, None]
    n = jnp.arange(SUB_LEN, dtype=jnp.int32)[None, :]
    ang = (((2 * f + 1) * n) % (2 * SUB_FFT)).astype(f32) * (math.pi / SUB_FFT)
    c, s = jnp.cos(ang), jnp.sin(ang)
    tc, ts = c.astype(bf16), (-s).astype(bf16)
    scale = 2.0 / SUB_FFT
    ic = (scale * c).T.astype(bf16)
    isn = (-scale * s).T.astype(bf16)
    return tc, ts, ic, isn


def _longconv_kernel(z_ref, kre_ref, kim_ref, tc_ref, ts_ref, ic_ref, is_ref, o_ref,
                     zre, zim, wre, wim, *, phases):
    slabs = z_ref.shape[1]
    for q in range(phases):
        zq = jnp.concatenate(
            [z_ref[0, s, pl.ds(q, SUB_LEN, stride=phases), :] for s in range(slabs)], axis=1).astype(bf16)
        zre[q] = _dot(tc_ref[...], zq)
        zim[q] = _dot(ts_ref[...], zq)

    def pointwise(p, slot):
        for r in range(SUB_LEN // PW_ROWS):
            rs = slice(r * PW_ROWS, (r + 1) * PW_ROWS)
            are = jnp.zeros((PW_ROWS, zre.shape[2]), f32)
            aim = jnp.zeros((PW_ROWS, zre.shape[2]), f32)
            for q in range(phases):
                kd = jnp.where(p >= q, p - q, phases + q - p)
                kr, ki = kre_ref[kd, rs, :], kim_ref[kd, rs, :]
                xr, xi = zre[q, rs, :], zim[q, rs, :]
                are = are + (kr * xr - ki * xi)
                aim = aim + (kr * xi + ki * xr)
            wre[slot, rs, :] = are.astype(bf16)
            wim[slot, rs, :] = aim.astype(bf16)

    def inverse(p, slot):
        yp = _dot(ic_ref[...], wre[slot]) + _dot(is_ref[...], wim[slot])
        for s in range(slabs):
            o_ref[0, s, pl.ds(p, SUB_LEN, stride=phases), :] = yp[:, s * LANES:(s + 1) * LANES]

    pointwise(0, 0)

    def body(p, carry):
        pointwise(p, p % 2)
        inverse(p - 1, (p - 1) % 2)
        return carry

    lax.fori_loop(1, phases, body, 0)
    inverse(phases - 1, (phases - 1) % 2)


def _longconv(z4, kre, kim, tcf, tsf, ic, isn):
    b, slabs, l, _ = z4.shape
    phases = l // SUB_LEN
    sb = 2
    cb = sb * LANES
    zspec = pl.BlockSpec((1, sb, l, LANES), lambda j, bi: (bi, j, 0, 0))
    spectra_bytes = 2 * (2 * phases) * SUB_LEN * cb * 4
    kspec = pl.BlockSpec((2 * phases, SUB_LEN, cb), lambda j, bi: (0, 0, j),
                         pipeline_mode=pl.Buffered(1 if 2 * spectra_bytes > VMEM_LIMIT // 2 else 2))
    tspec = _const_spec(tcf.shape)
    return pl.pallas_call(
        functools.partial(_longconv_kernel, phases=phases),
        grid=(slabs // sb, b),
        in_specs=[zspec, kspec, kspec, tspec, tspec, tspec, tspec], out_specs=zspec,
        out_shape=jax.ShapeDtypeStruct(z4.shape, f32),
        scratch_shapes=[pltpu.VMEM((phases, SUB_LEN, cb), f32), pltpu.VMEM((phases, SUB_LEN, cb), f32),
                        pltpu.VMEM((2, SUB_LEN, cb), bf16), pltpu.VMEM((2, SUB_LEN, cb), bf16)],
        compiler_params=_cparams("parallel", "parallel"), name="longconv",
    )(z4, kre.reshape(2 * phases, SUB_LEN, HY_WIDTH), kim.reshape(2 * phases, SUB_LEN, HY_WIDTH),
      tcf, tsf, ic, isn)


SSD_ROWS_PER_DIR = 5 * SSM_HEADS


def _ssd_bidir_kernel(xbc_ref, dtt_ref, dtb_ref, al_ref, ex_ref, gate_ref, dskip_ref, y_ref,
                      smat, rtmat, yacc, stash, hf_ref, hb_ref):
    q = SSM_CHUNK
    nh = SSM_HEADS
    n = xbc_ref.shape[1] // q
    half = n // 2
    rows_dir = SSD_ROWS_PER_DIR
    hp = (nh // SSM_GROUPS) * SSM_HEAD_DIM
    log2e = 1.0 / math.log(2.0)

    ii = lax.broadcasted_iota(jnp.int32, (q, q), 0)
    jj = lax.broadcasted_iota(jnp.int32, (q, q), 1)
    lower, upper = ii >= jj, ii <= jj
    first_head = lax.broadcasted_iota(jnp.int32, (q, LANES), 1) < SSM_HEAD_DIM

    dt16 = _softplus(dtt_ref[0] + dtb_ref[...])
    da16 = dt16 * (-jnp.exp(al_ref[...]))
    stack = lambda a: jnp.concatenate([a[:, c * q:(c + 1) * q] for c in range(n)], axis=0)
    smat[:, 2 * rows_dir:, :] = jnp.zeros((n, q - 2 * rows_dir, q), f32)
    for d in range(2):
        m_sum = jnp.where(upper if d == 0 else lower, 1.0, 0.0).astype(bf16)
        dt = stack(dt16[d * nh:(d + 1) * nh])
        cs2 = sum(_dot(part, m_sum) for part in _split3(stack(da16[d * nh:(d + 1) * nh]))) * log2e
        edge = cs2[:, q - 1:q] if d == 0 else cs2[:, 0:1]
        e = jnp.exp2(cs2)
        dd = dt * jnp.exp2(edge - cs2)
        e_hi = e.astype(bf16).astype(f32)
        dd_hi = dd.astype(bf16).astype(f32)
        for j, a in enumerate((cs2, e_hi, e - e_hi, dd_hi, dd - dd_hi)):
            smat[:, d * rows_dir + j * nh:d * rows_dir + (j + 1) * nh, :] = a.reshape(n, nh, q)
        rtmat[:, d * nh:(d + 1) * nh, :] = (cs2 - jnp.log2(dt)).reshape(n, nh, q)
    hf_ref[...] = jnp.zeros_like(hf_ref)
    hb_ref[...] = jnp.zeros_like(hb_ref)

    def chunk_rows(c):
        return pl.ds(pl.multiple_of(c * q, q), q)

    def in_chunk(c, d, h_ref):
        rows = chunk_rows(c)
        xsb = xbc_ref[0, rows, 0:SSM_WIDTH]
        bm = xbc_ref[0, rows, SSM_WIDTH:SSM_WIDTH + BC_WIDTH]
        cm = xbc_ref[0, rows, SSM_WIDTH + BC_WIDTH:XBC_WIDTH]
        xs = xsb.astype(f32)
        cols = smat[c].T
        wide = _dot(cols.astype(bf16), ex_ref[...])
        e_f, dd_f = wide[:, 0:SSM_WIDTH], wide[:, SSM_WIDTH:2 * SSM_WIDTH]
        e_b, dd_b = wide[:, 2 * SSM_WIDTH:3 * SSM_WIDTH], wide[:, 3 * SSM_WIDTH:]
        xdd_f = (xs * dd_f).astype(bf16)
        xdd_b = (xs * dd_b).astype(bf16)
        rt = rtmat[c]
        cstack = jnp.concatenate([cm[:, :SSM_STATE], cm[:, SSM_STATE:]], axis=0)
        bstack = jnp.concatenate([bm[:, :SSM_STATE], bm[:, SSM_STATE:]], axis=0)
        cb_all = _dot_nt(cstack, bstack)
        ys, sts_f, sts_b = [], [], []
        for g in range(SSM_GROUPS):
            bg = bm[:, g * SSM_STATE:(g + 1) * SSM_STATE]
            gl = slice(g * hp, (g + 1) * hp)
            cb = cb_all[g * q:(g + 1) * q, g * q:(g + 1) * q]
            st = _dot_tn(bg, jnp.concatenate([xdd_f[:, gl], xdd_b[:, gl]], axis=1))
            sts_f.append(st[:, :hp])
            sts_b.append(st[:, hp:])
            for pr in range(nh // SSM_GROUPS // 2):
                ms = []
                for hd in (g * (nh // SSM_GROUPS) + 2 * pr, g * (nh // SSM_GROUPS) + 2 * pr + 1):
                    df = cols[:, hd:hd + 1] - rt[hd:hd + 1, :]
                    db = cols[:, rows_dir + hd:rows_dir + hd + 1] - rt[nh + hd:nh + hd + 1, :]
                    lm = jnp.exp2(jnp.where(lower, df, -1e30)) + jnp.exp2(jnp.where(upper, db, -1e30))
                    ms.append((cb * lm).astype(bf16))
                xp = xsb[:, g * hp + pr * LANES:g * hp + (pr + 1) * LANES]
                zero = jnp.zeros_like(xp)
                rhs = jnp.concatenate([jnp.where(first_head, xp, zero), jnp.where(first_head, zero, xp)], axis=0)
                ys.append(_dot(jnp.concatenate(ms, axis=1), rhs))
        st = (jnp.concatenate(sts_f, axis=1), jnp.concatenate(sts_b, axis=1))
        yoff = carried(c, h_ref, (e_f, e_b)[d], st[d], (q - 1, 0)[d])
        yacc[rows, :] = jnp.concatenate(ys, axis=1) + xs * dskip_ref[...] + yoff
        return st[1 - d]

    def expand_e(c, d):
        cols = smat[c].T
        return _dot(cols.astype(bf16), ex_ref[:, 2 * d * SSM_WIDTH:(2 * d + 1) * SSM_WIDTH])

    def carried(c, h_ref, e_x, st, edge_row):
        cm = xbc_ref[0, chunk_rows(c), SSM_WIDTH + BC_WIDTH:XBC_WIDTH]
        h = h_ref[...]
        hb = h.astype(bf16)
        zh = jnp.zeros((SSM_STATE, hp), bf16)
        h_blocks = jnp.concatenate([jnp.concatenate([hb[:, :hp], zh], axis=1),
                                    jnp.concatenate([zh, hb[:, hp:]], axis=1)], axis=0)
        h_ref[...] = h * e_x[edge_row:edge_row + 1, :] + st
        return _dot(cm, h_blocks) * e_x

    def finish(c, y):
        rows = chunk_rows(c)
        gated = y * _silu(gate_ref[0, rows, :].astype(f32))
        y_ref[0, rows, :] = _rms(gated).astype(y_ref.dtype)

    def first_half(k, carry):
        m = n - 1 - k
        stash[k] = in_chunk(k, 0, hf_ref)
        stash[m] = in_chunk(m, 1, hb_ref)
        return carry

    def second_half(k, carry):
        m = n - 1 - k
        finish(k, yacc[chunk_rows(k), :] + carried(k, hf_ref, expand_e(k, 0), stash[k], q - 1))
        finish(m, yacc[chunk_rows(m), :] + carried(m, hb_ref, expand_e(m, 1), stash[m], 0))
        return carry

    lax.fori_loop(0, half, first_half, 0, unroll=half)
    lax.fori_loop(half, n, second_half, 0, unroll=half)


def _ssd_bidir(xbc, dtt, gate, dtb, alog, expand, dskip, layer):
    b, l, _ = xbc.shape
    n = l // SSM_CHUNK
    seq = lambda w: pl.BlockSpec((1, l, w), lambda bi: (bi, 0, 0))
    return pl.pallas_call(
        _ssd_bidir_kernel, grid=(b,),
        in_specs=[seq(XBC_WIDTH), pl.BlockSpec((1, DT_WIDTH, l), lambda bi: (bi, 0, 0)),
                  _layer_spec(dtb, layer), _layer_spec(alog, layer), _const_spec(expand.shape), seq(SSM_WIDTH),
                  _layer_spec(dskip, layer)],
        out_specs=seq(SSM_WIDTH), out_shape=jax.ShapeDtypeStruct((b, l, SSM_WIDTH), bf16),
        scratch_shapes=[pltpu.VMEM((n, SSM_CHUNK, SSM_CHUNK), f32),
                        pltpu.VMEM((n, DT_WIDTH, SSM_CHUNK), f32),
                        pltpu.VMEM((l, SSM_WIDTH), f32),
                        pltpu.VMEM((n, SSM_STATE, SSM_WIDTH), f32),
                        pltpu.VMEM((SSM_STATE, SSM_WIDTH), f32), pltpu.VMEM((SSM_STATE, SSM_WIDTH), f32)],
        compiler_params=_cparams("parallel"), name="ssd_bidir",
    )(xbc, dtt, dtb, alog, expand, gate, dskip)


def _ssd_bidir_expand_matrix():
    row = jnp.arange(SSM_CHUNK, dtype=jnp.int32)[:, None]
    lane = jnp.arange(4 * SSM_WIDTH, dtype=jnp.int32)[None, :]
    head = (lane % SSM_WIDTH) // SSM_HEAD_DIM
    quantity = (lane // SSM_WIDTH) % 2
    direction = lane // (2 * SSM_WIDTH)
    base = direction * SSD_ROWS_PER_DIR + SSM_HEADS + 2 * SSM_HEADS * quantity
    hit = jnp.logical_or(row == base + head, row == base + SSM_HEADS + head)
    return hit.astype(bf16)


def _kv_kernel(mem_ref, wkv_ref, k_ref, v_ref):
    mn = _rms(mem_ref[0]).astype(bf16)
    k_ref[0] = _dot(mn, wkv_ref[:, :D_MODEL]).astype(bf16)
    v_ref[0] = _dot(mn, wkv_ref[:, D_MODEL:]).astype(bf16)


def _kv_proj(mem, wkv, layer):
    b, m, d = mem.shape
    spec = pl.BlockSpec((1, m, d), lambda bi: (bi, 0, 0))
    return pl.pallas_call(
        _kv_kernel, grid=(b,), in_specs=[spec, _layer_spec(wkv, layer)],
        out_specs=(spec, spec), out_shape=(jax.ShapeDtypeStruct((b, m, d), bf16),) * 2,
        compiler_params=_cparams("parallel"), name="kv_proj",
    )(mem, wkv)


def _mix_attn_kernel(x_ref, x0_ref, z_ref, zc_ref, ys_ref, hb_ref, wout_ref, wq_ref,
                     k_ref, v_ref, wo_ref, o_ref):
    z = jnp.concatenate([z_ref[0, s] for s in range(HY_SLABS)], axis=1)
    zc = jnp.concatenate([zc_ref[0, s] for s in range(HY_SLABS)], axis=1)
    y_hy = _rms(x0_ref[0].astype(f32) * (zc + z * hb_ref[...])).astype(bf16)
    x1 = x_ref[0] + _dot(jnp.concatenate([y_hy, ys_ref[0]], axis=1), wout_ref[...])
    xn = _rms(x1).astype(bf16)
    qa = _dot(xn, wq_ref[...]).astype(bf16)
    heads = []
    for h in range(XA_HEADS):
        hs = slice(h * XA_HEAD_DIM, (h + 1) * XA_HEAD_DIM)
        s = _dot_nt(qa[:, hs], k_ref[0, :, hs])
        e = jnp.exp(s - jnp.max(s, axis=-1, keepdims=True))
        den = jnp.sum(e, axis=-1, keepdims=True)
        heads.append(_dot(e.astype(bf16), v_ref[0, :, hs]) / den)
    o = jnp.concatenate(heads, axis=1).astype(bf16)
    o_ref[0] = x1 + _dot(o, wo_ref[...])


def _mix_attn(x, x0, z4, zc4, ys, hbias, wout, wq, k, v, wo, layer):
    b, l, d = x.shape
    tm = ROW_TILE
    row = lambda w: pl.BlockSpec((1, tm, w), lambda bi, i: (bi, i, 0))
    slab = pl.BlockSpec((1, HY_SLABS, tm, LANES), lambda bi, i: (bi, 0, i, 0))
    memspec = pl.BlockSpec((1, N_MEM, d), lambda bi, i: (bi, 0, 0))
    return pl.pallas_call(
        _mix_attn_kernel, grid=(b, l // tm),
        in_specs=[row(d), row(HY_WIDTH), slab, slab, row(SSM_WIDTH), _layer_spec(hbias, layer),
                  _layer_spec(wout, layer), _layer_spec(wq, layer), memspec, memspec, _layer_spec(wo, layer)],
        out_specs=row(d), out_shape=jax.ShapeDtypeStruct((b, l, d), f32),
        compiler_params=_cparams("parallel", "parallel"), name="mix_attn",
    )(x, x0, z4, zc4, ys, hbias, wout, wq, k, v, wo)


def _mlp_kernel(x_ref, wup_ref, wdn_ref, fw_ref, o_ref, *, final):
    x = x_ref[0]
    xn = _rms(x).astype(bf16)
    acc = x
    for c in range(D_FF // D_MODEL):
        cs = slice(c * D_MODEL, (c + 1) * D_MODEL)
        hdn = jnp.maximum(_dot(xn, wup_ref[:, cs]), 0.0)
        acc = acc + _dot((hdn * hdn).astype(bf16), wdn_ref[cs, :])
    o_ref[0] = _rms(acc, fw_ref[...]) if final else acc


def _mlp(x, wup, wdn, fw, final, layer):
    b, l, d = x.shape
    tm = ROW_TILE
    row = pl.BlockSpec((1, tm, d), lambda bi, i: (bi, i, 0))
    wspec = lambda a: _layer_spec(a, layer, pl.Buffered(1))
    return pl.pallas_call(
        functools.partial(_mlp_kernel, final=final), grid=(b, l // tm),
        in_specs=[row, wspec(wup), wspec(wdn), _const_spec(fw.shape)],
        out_specs=row, out_shape=jax.ShapeDtypeStruct((b, l, d), f32),
        compiler_params=_cparams("parallel", "parallel"), name="mlp",
    )(x, wup, wdn, fw)


def _trunk(x, mem, p, tables):
    tc, ts, ic, isn = tables
    seq_len = x.shape[1]
    for i in range(DEPTH):
        x0, z4, gate, xbc, dtt = _inproj(x, p["w_main"], p["w_dtt"], p["hy_conv_w"], p["hy_conv_b"],
                                         p["ssm_conv_w"], p["ssm_conv_b"], layer=i)
        kre, kim = _filter_spectra(seq_len, p["filter_mlp"], i, tc, ts)
        zc4 = _longconv(z4, kre, kim, tc, ts, ic, isn)
        ys = _ssd_bidir(xbc, dtt, gate, p["ssm_dt_bias"], p["ssm_A_log"], p["ssd_expand"], p["ssm_D_wide"], layer=i)
        k, v = _kv_proj(mem, p["w_kv"], layer=i)
        x = _mix_attn(x, x0, z4, zc4, ys, p["hy_bias"], p["w_out"], p["w_q"], k, v, p["w_o"], layer=i)
        x = _mlp(x, p["w_up"], p["w_down"], p["norm_final"], final=(i == DEPTH - 1), layer=i)
    return x


def kernel(x_prompt, x_sample, mem_prompt, mem_sample, norm_mix, w_in, hy_conv_w, hy_conv_b, hy_fw1, hy_fb1, hy_fw2, hy_fb2, hy_fw3, hy_fb3, hy_sin_freq, hy_bias, hy_norm, ssm_conv_w, ssm_conv_b, ssm_dt_bias, ssm_A_log, ssm_D, ssm_norm, w_out, norm_xattn, norm_mem, w_q, w_kv, w_o, norm_mlp, w_up, w_down, norm_final):
    gained = lambda g, w: (g[..., None] * w).astype(bf16)
    rows = lambda a: a[:, None, :]
    w_main = gained(norm_mix, w_in)
    p = dict(norm_final=norm_final.reshape(1, -1), ssd_expand=_ssd_bidir_expand_matrix(),
             hy_conv_w=hy_conv_w, hy_conv_b=rows(hy_conv_b), ssm_conv_w=ssm_conv_w, ssm_conv_b=rows(ssm_conv_b),
             hy_bias=rows(hy_bias), ssm_dt_bias=ssm_dt_bias.reshape(DEPTH, DT_WIDTH, 1),
             ssm_A_log=ssm_A_log.reshape(DEPTH, DT_WIDTH, 1), ssm_D_wide=rows(jnp.repeat(ssm_D, SSM_HEAD_DIM, axis=-1)),
             filter_mlp=_filter_mlp_params(hy_fw1, hy_fb1, hy_fw2, hy_fb2, hy_fw3, hy_fb3, hy_sin_freq),
             w_main=w_main, w_dtt=jnp.swapaxes(w_main[:, :, MAIN_IN:], 1, 2),
             w_out=gained(jnp.concatenate([hy_norm, ssm_norm], axis=-1), w_out),
             w_q=gained(norm_xattn * (XA_HEAD_DIM ** -0.5), w_q), w_kv=gained(norm_mem, w_kv),
             w_o=w_o.astype(bf16), w_up=gained(norm_mlp, w_up), w_down=w_down.astype(bf16))
    tables = _dft_tables()
    return (_trunk(x_prompt, mem_prompt, p, tables), _trunk(x_sample, mem_sample, p, tables))
```
